```python
import jax
import jax.numpy as jnp
from jax import lax
import numpy as np

D_MODEL = 1024
BATCH = 8
SEQ = 4096
DEPTH = 2
DEC_BATCH = 32
DEC_SEQ = 4
PAST_LEN = 16384
PAGE_SIZE = 128

D_A = D_MODEL // 4
D_B = D_MODEL // 4
D_C = D_MODEL // 2
D_MIX = D_A + D_B + D_C
HEAD_DIM = 64
N_HEADS_C = D_C // HEAD_DIM
CONV_A_WIDTH = 31
CONV_B_WIDTH = 3
FFN_CONV_WIDTH = 3
D_FF = 11 * D_MODEL // 4
DILATED_PATTERNS = ((128, 1), (512, 4), (2048, 16))
MAX_WINDOW = 2048
Q_BLOCK = 128
P_IN = 2 * D_A + 3 * D_B + 3 * D_C
EPS = 1e-6
NEG = -1e30

kernel_name = 'hybrid_conformer_shortconv_dilated_attn_step'


def _rmsnorm(x, g):
    xf = x.astype(jnp.float32)
    y = xf * lax.rsqrt(jnp.mean(xf * xf, axis=-1, keepdims=True) + EPS)
    return (y * g.astype(jnp.float32)).astype(x.dtype)


def _layernorm(x, g, b):
    xf = x.astype(jnp.float32)
    mu = jnp.mean(xf, axis=-1, keepdims=True)
    xc = xf - mu
    var = jnp.mean(xc * xc, axis=-1, keepdims=True)
    return (xc * lax.rsqrt(var + EPS) * g.astype(jnp.float32) + b.astype(jnp.float32)).astype(x.dtype)


def _causal_dwconv(u, state, w):
    k = w.shape[0]
    xp = jnp.concatenate([state.astype(u.dtype), u], axis=1)
    y = lax.conv_general_dilated(
        xp, w[:, None, :].astype(u.dtype), window_strides=(1,), padding='VALID',
        dimension_numbers=('NWC', 'WIO', 'NWC'), feature_group_count=u.shape[-1])
    return y, xp[:, xp.shape[1] - (k - 1):]


def _dilated_attention(q, k, v, q_idx):
    scale = HEAD_DIM ** -0.5
    lses = []
    outs = []
    for window, dil in DILATED_PATTERNS:
        dist = jnp.arange(0, window + 1, dil, dtype=jnp.int32)
        idx = q_idx[:, None] - dist[None, :]
        valid = idx >= 0
        idx = jnp.maximum(idx, 0)
        kg = jnp.take(k, idx, axis=1)
        vg = jnp.take(v, idx, axis=1)
        s = jnp.einsum('bthd,btkhd->bthk', q, kg).astype(jnp.float32) * scale
        s = jnp.where(valid[None, :, None, :], s, NEG)
        lse = jax.nn.logsumexp(s, axis=-1)
        p = jnp.exp(s - lse[..., None])
        outs.append(jnp.einsum('bthk,btkhd->bthd', p.astype(v.dtype), vg).astype(jnp.float32))
        lses.append(lse)
    wts = jax.nn.softmax(jnp.stack(lses), axis=0)
    o = jnp.sum(wts[..., None] * jnp.stack(outs), axis=0)
    return o.astype(q.dtype)


def _layer(x, c, st_a, st_b, st_f, k_past, v_past, lp, is_prompt):
    bsz, t, _ = x.shape
    mod = jax.nn.silu(c) @ lp['w_ada'] + lp['b_ada']
    sh_m, sc_m, ga_m, sh_f, sc_f, ga_f = [m[:, None, :] for m in jnp.split(mod, 6, axis=-1)]

    h = _rmsnorm(x, lp['g_pre_mix']) * (1 + sc_m) + sh_m
    z = h @ lp['w_in']
    splits = [D_A, 2 * D_A, 2 * D_A + D_B, 2 * D_A + 2 * D_B, 2 * D_A + 3 * D_B,
              2 * D_A + 3 * D_B + D_C, 2 * D_A + 3 * D_B + 2 * D_C]
    a_val, a_gate, b_x, b_bg, b_cg, q, k, v = jnp.split(z, splits, axis=-1)

    a = a_val * jax.nn.sigmoid(a_gate)
    a, new_a = _causal_dwconv(a, st_a, lp['conv_a_w'])
    a = jax.nn.silu(_layernorm(a + lp['conv_a_b'], lp['ln_a_g'], lp['ln_a_b']))

    u = b_cg * b_x
    u, new_b = _causal_dwconv(u, st_b, lp['conv_b_w'])
    bo = b_bg * u

    q = q.reshape(bsz, t, N_HEADS_C, HEAD_DIM)
    k = k.reshape(bsz, t, N_HEADS_C, HEAD_DIM)
    v = v.reshape(bsz, t, N_HEADS_C, HEAD_DIM)
    if is_prompt:
        nb = t // Q_BLOCK
        qb = q.reshape(bsz, nb, Q_BLOCK, N_HEADS_C, HEAD_DIM).swapaxes(0, 1)

        def _block(args):
            qi, i = args
            return _dilated_attention(qi, k, v, i * Q_BLOCK + jnp.arange(Q_BLOCK, dtype=jnp.int32))

        o = lax.map(_block, (qb, jnp.arange(nb, dtype=jnp.int32)))
        o = o.swapaxes(0, 1).reshape(bsz, t, D_C)
        keep = min(MAX_WINDOW, t)
        new_k = k[:, t - keep:]
        new_v = v[:, t - keep:]
    else:
        k_all = jnp.concatenate([k_past.astype(k.dtype), k], axis=1)
        v_all = jnp.concatenate([v_past.astype(v.dtype), v], axis=1)
        q_idx = k_past.shape[1] + jnp.arange(t, dtype=jnp.int32)
        o = _dilated_attention(q, k_all, v_all, q_idx).reshape(bsz, t, D_C)
        new_k = k
        new_v = v

    mix = jnp.concatenate([_rmsnorm(a, lp['g_out_a']), _rmsnorm(bo, lp['g_out_b']),
                           _rmsnorm(o, lp['g_out_c'])], axis=-1)
    x = x + ga_m * _rmsnorm(mix @ lp['w_o'], lp['g_post_mix'])

    h = _rmsnorm(x, lp['g_pre_ffn']) * (1 + sc_f) + sh_f
    g, new_f = _causal_dwconv(h @ lp['w_gate'], st_f, lp['conv_f_w'])
    f = jax.nn.silu(g) * (h @ lp['w_up'])
    x = x + ga_f * _rmsnorm(f @ lp['w_down'], lp['g_post_ffn'])
    return x, new_k, new_v, new_a, new_b, new_f


def setup_inputs(seed: int = 0) -> dict:
    key = jax.random.key(seed)
    ks = jax.random.split(key, 32)
    f32 = jnp.float32
    w_buf = min(MAX_WINDOW, PAST_LEN)

    def nrm(k, shape, scale=1.0):
        return jax.random.normal(k, shape, f32) * scale

    def gain(k, shape):
        return 1.0 + 0.02 * jax.random.normal(k, shape, f32)

    return {
        'x_prompt': nrm(ks[0], (BATCH, SEQ, D_MODEL)),
        'x_sample': nrm(ks[1], (DEC_BATCH, DEC_SEQ, D_MODEL)),
        'cache_k': nrm(ks[2], (DEPTH, DEC_BATCH, w_buf, N_HEADS_C, HEAD_DIM)),
        'cache_v': nrm(ks[3], (DEPTH, DEC_BATCH, w_buf, N_HEADS_C, HEAD_DIM)),
        'state_conv_a': nrm(ks[4], (DEPTH, DEC_BATCH, CONV_A_WIDTH - 1, D_A), 0.5),
        'state_conv_b': nrm(ks[5], (DEPTH, DEC_BATCH, CONV_B_WIDTH - 1, D_B), 0.5),
        'state_ffn_conv': nrm(ks[6], (DEPTH, DEC_BATCH, FFN_CONV_WIDTH - 1, D_FF), 0.5),
        'c_prompt': nrm(ks[7], (BATCH, D_MODEL)),
        'c_sample': nrm(ks[8], (DEC_BATCH, D_MODEL)),
        'w_ada': nrm(ks[9], (DEPTH, D_MODEL, 6 * D_MODEL), D_MODEL ** -0.5),
        'b_ada': nrm(ks[10], (DEPTH, 6 * D_MODEL), 0.02),
        'g_pre_mix': gain(ks[11], (DEPTH, D_MODEL)),
        'w_in': nrm(ks[12], (DEPTH, D_MODEL, P_IN), D_MODEL ** -0.5),
        'conv_a_w': nrm(ks[13], (DEPTH, CONV_A_WIDTH, D_A), CONV_A_WIDTH ** -0.5),
        'conv_a_b': nrm(ks[14], (DEPTH, D_A), 0.02),
        'ln_a_g': gain(ks[15], (DEPTH, D_A)),
        'ln_a_b': nrm(ks[16], (DEPTH, D_A), 0.02),
        'conv_b_w': nrm(ks[17], (DEPTH, CONV_B_WIDTH, D_B), CONV_B_WIDTH ** -0.5),
        'g_out_a': gain(ks[18], (DEPTH, D_A)),
        'g_out_b': gain(ks[19], (DEPTH, D_B)),
        'g_out_c': gain(ks[20], (DEPTH, D_C)),
        'w_o': nrm(ks[21], (DEPTH, D_MIX, D_MODEL), D_MIX ** -0.5),
        'g_post_mix': gain(ks[22], (DEPTH, D_MODEL)),
        'g_pre_ffn': gain(ks[23], (DEPTH, D_MODEL)),
        'w_gate': nrm(ks[24], (DEPTH, D_MODEL, D_FF), D_MODEL ** -0.5),
        'w_up': nrm(ks[25], (DEPTH, D_MODEL, D_FF), D_MODEL ** -0.5),
        'conv_f_w': nrm(ks[26], (DEPTH, FFN_CONV_WIDTH, D_FF), FFN_CONV_WIDTH ** -0.5),
        'w_down': nrm(ks[27], (DEPTH, D_FF, D_MODEL), D_FF ** -0.5),
        'g_post_ffn': gain(ks[28], (DEPTH, D_MODEL)),
    }


def reference(x_prompt, x_sample, cache_k, cache_v, state_conv_a, state_conv_b, state_ffn_conv,
              c_prompt, c_sample, w_ada, b_ada, g_pre_mix, w_in, conv_a_w, conv_a_b, ln_a_g, ln_a_b,
              conv_b_w, g_out_a, g_out_b, g_out_c, w_o, g_post_mix, g_pre_ffn, w_gate, w_up,
              conv_f_w, w_down, g_post_ffn):
    xp = x_prompt
    xs = x_sample
    kp_l, vp_l, ks_l, vs_l = [], [], [], []
    ap_l, as_l, bp_l, bs_l, fp_l, fs_l = [], [], [], [], [], []
    for l in range(DEPTH):
        lp = {
            'w_ada': w_ada[l], 'b_ada': b_ada[l], 'g_pre_mix': g_pre_mix[l], 'w_in': w_in[l],
            'conv_a_w': conv_a_w[l], 'conv_a_b': conv_a_b[l], 'ln_a_g': ln_a_g[l], 'ln_a_b': ln_a_b[l],
            'conv_b_w': conv_b_w[l], 'g_out_a': g_out_a[l], 'g_out_b': g_out_b[l], 'g_out_c': g_out_c[l],
            'w_o': w_o[l], 'g_post_mix': g_post_mix[l], 'g_pre_ffn': g_pre_ffn[l],
            'w_gate': w_gate[l], 'w_up': w_up[l], 'conv_f_w': conv_f_w[l], 'w_down': w_down[l],
            'g_post_ffn': g_post_ffn[l],
        }
        z_a = jnp.zeros((xp.shape[0], CONV_A_WIDTH - 1, D_A), xp.dtype)
        z_b = jnp.zeros((xp.shape[0], CONV_B_WIDTH - 1, D_B), xp.dtype)
        z_f = jnp.zeros((xp.shape[0], FFN_CONV_WIDTH - 1, D_FF), xp.dtype)
        xp, kp, vp, ap, bp, fp = _layer(xp, c_prompt, z_a, z_b, z_f, None, None, lp, True)
        xs, ks_, vs_, as_, bs_, fs_ = _layer(xs, c_sample, state_conv_a[l], state_conv_b[l],
                                            state_ffn_conv[l], cache_k[l], cache_v[l], lp, False)
        kp_l.append(kp); vp_l.append(vp); ks_l.append(ks_); vs_l.append(vs_)
        ap_l.append(ap); as_l.append(as_); bp_l.append(bp); bs_l.append(bs_)
        fp_l.append(fp); fs_l.append(fs_)
    return (xp, xs,
            jnp.stack(kp_l), jnp.stack(vp_l), jnp.stack(ks_l), jnp.stack(vs_l),
            jnp.stack(ap_l), jnp.stack(as_l), jnp.stack(bp_l), jnp.stack(bs_l),
            jnp.stack(fp_l), jnp.stack(fs_l))
```

```python
import functools

import jax
import jax.numpy as jnp
from jax import lax
from jax.experimental import pallas as pl
from jax.experimental.pallas import tpu as pltpu

F32 = jnp.float32
BF16 = jnp.bfloat16

HEAD_DIM = 64
LANES = 128
SUBLANES = 8
CONV_A_WIDTH = 31
CONV_B_WIDTH = 3
FFN_CONV_WIDTH = 3
DILATED_PATTERNS = ((128, 1), (512, 4), (2048, 16))
Q_BLK = 128
EPS = 1e-6
NEG = -1e30
VMEM_LIMIT = 56 * 1024 * 1024
PROMPT_TILE_ROWS = 512


def _round_up(n, m):
    return (n + m - 1) // m * m


def _rms(x, g):
    return x * lax.rsqrt(jnp.mean(x * x, axis=-1, keepdims=True) + EPS) * g


def _sigmoid(x):
    return 1.0 / (1.0 + jnp.exp(-x))


def _dot(a, b):
    return jnp.dot(a, b, preferred_element_type=F32)


def _dot_nt(a, b):
    return lax.dot_general(a, b, (((1,), (1,)), ((), ())), preferred_element_type=F32)


def _const_spec(shape):
    zeros = (0,) * len(shape)
    return pl.BlockSpec(shape, lambda *_: zeros, pipeline_mode=pl.Buffered(1))


def _ada_kernel(c_ref, w_ref, b_ref, o_ref):
    c = c_ref[...]
    s = (c * _sigmoid(c)).astype(BF16)
    o_ref[0] = _dot(s, w_ref[0].astype(BF16)) + b_ref[0]


def _ada_call(c_all, w_ada, b_ada):
    depth, d, n = w_ada.shape
    rows = c_all.shape[0]
    tn = n // 4
    return pl.pallas_call(
        _ada_kernel,
        grid=(depth, n // tn),
        in_specs=[
            pl.BlockSpec((rows, d), lambda l, j: (0, 0)),
            pl.BlockSpec((1, d, tn), lambda l, j: (l, 0, j)),
            pl.BlockSpec((1, 1, tn), lambda l, j: (l, 0, j)),
        ],
        out_specs=pl.BlockSpec((1, rows, tn), lambda l, j: (l, 0, j)),
        out_shape=jax.ShapeDtypeStruct((depth, rows, n), F32),
        compiler_params=pltpu.CompilerParams(
            dimension_semantics=("arbitrary", "arbitrary"), vmem_limit_bytes=VMEM_LIMIT),
        name="ada_mod",
    )(c_all, w_ada, b_ada.reshape(depth, 1, n))


def _conv_rows(buf_ref, w_ref, base, ntaps, shift, rows, chunk):
    outs = []
    for c0 in range(0, rows, chunk):
        acc = None
        for j in range(ntaps):
            term = buf_ref[pl.ds(base + c0 + j * shift, chunk), :] * w_ref[j:j + 1, :]
            acc = term if acc is None else acc + term
        outs.append(acc)
    return outs[0] if len(outs) == 1 else jnp.concatenate(outs, axis=0)


def _mixin_kernel(x_ref, mod_ref, gpre_ref, win_ref, caw_ref, cab_ref, lng_ref, lnb_ref, cbw_ref,
                  goa_ref, gob_ref, sta_ref, stb_ref,
                  mixab_ref, q_ref, k_ref, v_ref, na_ref, nb_ref,
                  abuf, ubuf, *, rows, shift, d_a, d_b, d_c, carry_a, carry_b, n_tiles):
    t = pl.program_id(1)

    @pl.when(t == 0)
    def _():
        abuf[0:carry_a, :] = sta_ref[0]
        ubuf[0:carry_b, :] = stb_ref[0]

    x = x_ref[0]
    h = (_rms(x, gpre_ref[...]) * (1.0 + mod_ref[0, 1]) + mod_ref[0, 0]).astype(BF16)

    za = _dot(h, win_ref[:, 0:2 * d_a])
    abuf[carry_a:carry_a + rows, :] = za[:, 0:d_a] * _sigmoid(za[:, d_a:2 * d_a])
    hist_a = (CONV_A_WIDTH - 1) * shift
    a = _conv_rows(abuf, caw_ref, carry_a - hist_a, CONV_A_WIDTH, shift, rows, min(rows, 64)) + cab_ref[...]
    mu = jnp.mean(a, axis=-1, keepdims=True)
    ac = a - mu
    var = jnp.mean(ac * ac, axis=-1, keepdims=True)
    a = ac * lax.rsqrt(var + EPS) * lng_ref[...] + lnb_ref[...]
    a = a * _sigmoid(a)
    mixab_ref[0, :, 0:d_a] = _rms(a, goa_ref[...]).astype(BF16)
    na_ref[0] = abuf[rows:rows + carry_a, :]

    o_b = 2 * d_a
    zb = _dot(h, win_ref[:, o_b:o_b + 3 * d_b])
    ubuf[carry_b:carry_b + rows, :] = zb[:, 2 * d_b:3 * d_b] * zb[:, 0:d_b]
    hist_b = (CONV_B_WIDTH - 1) * shift
    u = _conv_rows(ubuf, cbw_ref, carry_b - hist_b, CONV_B_WIDTH, shift, rows, rows)
    bo = zb[:, d_b:2 * d_b] * u
    mixab_ref[0, :, d_a:d_a + d_b] = _rms(bo, gob_ref[...]).astype(BF16)
    nb_ref[0] = ubuf[rows:rows + carry_b, :]

    o_c = o_b + 3 * d_b
    q_ref[0] = _dot(h, win_ref[:, o_c:o_c + d_c])
    k_ref[0] = _dot(h, win_ref[:, o_c + d_c:o_c + 2 * d_c])
    v_ref[0] = _dot(h, win_ref[:, o_c + 2 * d_c:o_c + 3 * d_c])

    if n_tiles > 1:
        abuf[0:carry_a, :] = abuf[rows:rows + carry_a, :]
        ubuf[0:carry_b, :] = ubuf[rows:rows + carry_b, :]


def _mixin_call(x, mod, st_a, st_b, lw, *, rows, shift):
    groups, n, d = x.shape
    rm = mod.shape[2]
    d_a = lw['conv_a_w'].shape[1]
    d_b = lw['conv_b_w'].shape[1]
    d_c = lw['g_out_c'].shape[1]
    p_in = lw['w_in'].shape[1]
    carry_a, carry_b = st_a.shape[1], st_b.shape[1]
    n_tiles = n // rows
    kern = functools.partial(_mixin_kernel, rows=rows, shift=shift, d_a=d_a, d_b=d_b, d_c=d_c,
                             carry_a=carry_a, carry_b=carry_b, n_tiles=n_tiles)
    tile = lambda w: pl.BlockSpec((1, rows, w), lambda g, t: (g, t, 0))
    per_group = lambda r, w: pl.BlockSpec((1, r, w), lambda g, t: (g, 0, 0))
    return pl.pallas_call(
        kern,
        grid=(groups, n_tiles),
        in_specs=[
            tile(d),
            pl.BlockSpec((1, 6, rm, d), lambda g, t: (g, 0, 0, 0)),
            _const_spec((1, d)),
            _const_spec((d, p_in)),
            _const_spec((CONV_A_WIDTH, d_a)),
            _const_spec((1, d_a)),
            _const_spec((1, d_a)),
            _const_spec((1, d_a)),
            _const_spec((CONV_B_WIDTH, d_b)),
            _const_spec((1, d_a)),
            _const_spec((1, d_b)),
            per_group(carry_a, d_a),
            per_group(carry_b, d_b),
        ],
        out_specs=[
            tile(d_a + d_b), tile(d_c), tile(d_c), tile(d_c),
            per_group(carry_a, d_a), per_group(carry_b, d_b),
        ],
        out_shape=[
            jax.ShapeDtypeStruct((groups, n, d_a + d_b), BF16),
            jax.ShapeDtypeStruct((groups, n, d_c), F32),
            jax.ShapeDtypeStruct((groups, n, d_c), F32),
            jax.ShapeDtypeStruct((groups, n, d_c), F32),
            jax.ShapeDtypeStruct((groups, carry_a, d_a), F32),
            jax.ShapeDtypeStruct((groups, carry_b, d_b), F32),
        ],
        scratch_shapes=[
            pltpu.VMEM((carry_a + rows, d_a), F32),
            pltpu.VMEM((carry_b + rows, d_b), F32),
        ],
        compiler_params=pltpu.CompilerParams(
            dimension_semantics=("arbitrary", "arbitrary"), vmem_limit_bytes=VMEM_LIMIT),
        name="mix_in",
    )(x, mod, lw['g_pre_mix'], lw['w_in'], lw['conv_a_w'], lw['conv_a_b'], lw['ln_a_g'], lw['ln_a_b'],
      lw['conv_b_w'], lw['g_out_a'], lw['g_out_b'], st_a, st_b)


def _attn_kernel(q_ref, k_ref, v_ref, o_ref, qs0, qs1, ks, vs, acc, mrun, lrun, bias, *, seq):
    n_blk = seq // Q_BLK
    lane = lax.broadcasted_iota(jnp.int32, (1, LANES), 1)
    head0 = lane < HEAD_DIM
    scale = HEAD_DIM ** -0.5

    row = lax.broadcasted_iota(jnp.int32, (Q_BLK, 2 * Q_BLK), 0)
    col = lax.broadcasted_iota(jnp.int32, (Q_BLK, 2 * Q_BLK), 1)
    band = jnp.logical_and(col >= row, col <= row + Q_BLK)
    bias[0] = jnp.where(band, 0.0, NEG)
    bias[1] = jnp.where(jnp.logical_and(band, col >= Q_BLK), 0.0, NEG)
    vs[0:Q_BLK, :] = jnp.zeros((Q_BLK, LANES), BF16)
    ks[0:Q_BLK, :] = jnp.zeros((Q_BLK, LANES), BF16)

    n_pat = len(DILATED_PATTERNS)
    for pat, (_, dil) in enumerate(DILATED_PATTERNS):
        blk_per_phase = n_blk // dil

        def rows_of(j, dil=dil, blk_per_phase=blk_per_phase):
            if dil == 1:
                return pl.ds(pl.multiple_of(j * Q_BLK, Q_BLK), Q_BLK)
            phase = j // blk_per_phase
            i = j - phase * blk_per_phase
            return pl.ds(phase + i * (Q_BLK * dil), Q_BLK, stride=dil)

        def split(j, carry, rows_of=rows_of):
            src = rows_of(j)
            dst = pl.ds(pl.multiple_of(j * Q_BLK, Q_BLK), Q_BLK)
            dst_kv = pl.ds(pl.multiple_of((j + 1) * Q_BLK, Q_BLK), Q_BLK)
            qv = q_ref.at[0][src, :] * scale
            qs0[dst, :] = jnp.where(head0, qv, 0.0).astype(BF16)
            qs1[dst, :] = jnp.where(head0, 0.0, qv).astype(BF16)
            ks[dst_kv, :] = k_ref.at[0][src, :].astype(BF16)
            vs[dst_kv, :] = v_ref.at[0][src, :].astype(BF16)
            return carry

        lax.fori_loop(0, n_blk, split, 0)

        def block(j, carry, pat=pat, rows_of=rows_of, blk_per_phase=blk_per_phase):
            qrows = pl.ds(pl.multiple_of(j * Q_BLK, Q_BLK), Q_BLK)
            krows = pl.ds(pl.multiple_of(j * Q_BLK, Q_BLK), 2 * Q_BLK)
            kb = ks[krows, :]
            vb = vs[krows, :]
            first = (j % blk_per_phase == 0).astype(jnp.int32)
            bb = bias[first]
            parts = []
            for qs in (qs0, qs1):
                s = _dot_nt(qs[qrows, :], kb) + bb
                m = jnp.max(s, axis=-1, keepdims=True)
                p = jnp.exp(s - m)
                l = jnp.sum(p, axis=-1, keepdims=True)
                parts.append((m, l, _dot(p.astype(BF16), vb)))
            (m0, l0, a0), (m1, l1, a1) = parts
            m_new = jnp.where(head0, m0, m1)
            l_new = jnp.where(head0, l0, l1)
            a_new = jnp.where(head0, a0, a1)
            dst = rows_of(j)
            if pat > 0:
                m_old = mrun[dst, :]
                m_tot = jnp.maximum(m_old, m_new)
                w_old = jnp.exp(m_old - m_tot)
                w_new = jnp.exp(m_new - m_tot)
                a_new = w_old * acc[dst, :] + w_new * a_new
                l_new = w_old * lrun[dst, :] + w_new * l_new
                m_new = m_tot
            if pat < n_pat - 1:
                mrun[dst, :] = m_new
                lrun[dst, :] = l_new
                acc[dst, :] = a_new
            else:
                o_ref.at[0][dst, :] = a_new / l_new
            return carry

        lax.fori_loop(0, n_blk, block, 0)


def _attn_call(q, k, v):
    bsz, seq, d_c = q.shape
    assert seq % (Q_BLK * DILATED_PATTERNS[-1][1]) == 0
    spec = pl.BlockSpec((1, seq, LANES), lambda b, h: (b, 0, h))
    return pl.pallas_call(
        functools.partial(_attn_kernel, seq=seq),
        grid=(bsz, d_c // LANES),
        in_specs=[spec, spec, spec],
        out_specs=spec,
        out_shape=jax.ShapeDtypeStruct((bsz, seq, d_c), F32),
        scratch_shapes=[
            pltpu.VMEM((seq, LANES), BF16),
            pltpu.VMEM((seq, LANES), BF16),
            pltpu.VMEM((seq + Q_BLK, LANES), BF16),
            pltpu.VMEM((seq + Q_BLK, LANES), BF16),
            pltpu.VMEM((seq, LANES), F32),
            pltpu.VMEM((seq, LANES), F32),
            pltpu.VMEM((seq, LANES), F32),
            pltpu.VMEM((2, Q_BLK, 2 * Q_BLK), F32),
        ],
        compiler_params=pltpu.CompilerParams(
            dimension_semantics=("arbitrary", "arbitrary"), vmem_limit_bytes=VMEM_LIMIT),
        name="attn_prompt",
    )(q, k, v)


def _multiplicity(dist):
    cnt = jnp.zeros(dist.shape, F32)
    for window, dil in DILATED_PATTERNS:
        hit = jnp.logical_and(dist >= 0, jnp.logical_and(dist <= window, dist % dil == 0))
        cnt = cnt + jnp.where(hit, 1.0, 0.0)
    return cnt


def _attn_sample_kernel(q_ref, kn_ref, vn_ref, ck_ref, cv_ref, o_ref, *, n_q, n_heads, w_buf):
    d_c = n_heads * HEAD_DIM
    n_rows = n_heads * n_q
    scale = HEAD_DIM ** -0.5
    q = q_ref[0] * scale
    q_rep = jnp.concatenate([q] * n_heads, axis=0)
    row_head = lax.broadcasted_iota(jnp.int32, (n_rows, d_c), 0) // n_q
    lane_head = lax.broadcasted_iota(jnp.int32, (n_rows, d_c), 1) // HEAD_DIM
    own = row_head == lane_head
    q_bd = jnp.where(own, q_rep, 0.0).astype(BF16)

    def q_index(shape):
        r = lax.broadcasted_iota(jnp.int32, shape, 0)
        return w_buf + r - (r // n_q) * n_q

    s_c = _dot_nt(q_bd, ck_ref[0, 0].astype(BF16))
    key_c = lax.broadcasted_iota(jnp.int32, (n_rows, w_buf), 1)
    cnt_c = _multiplicity(q_index((n_rows, w_buf)) - key_c)
    s_c = jnp.where(cnt_c > 0.0, s_c, NEG)
    s_n = _dot_nt(q_bd, kn_ref[0].astype(BF16))
    key_n = w_buf + lax.broadcasted_iota(jnp.int32, (n_rows, n_q), 1)
    cnt_n = _multiplicity(q_index((n_rows, n_q)) - key_n)
    s_n = jnp.where(cnt_n > 0.0, s_n, NEG)

    m = jnp.maximum(jnp.max(s_c, axis=-1, keepdims=True), jnp.max(s_n, axis=-1, keepdims=True))
    p_c = cnt_c * jnp.exp(s_c - m)
    p_n = cnt_n * jnp.exp(s_n - m)
    l = jnp.sum(p_c, axis=-1, keepdims=True) + jnp.sum(p_n, axis=-1, keepdims=True)
    o_full = _dot(p_c.astype(BF16), cv_ref[0, 0].astype(BF16)) + _dot(p_n.astype(BF16), vn_ref[0].astype(BF16))
    o_full = jnp.where(own, o_full / l, 0.0)
    o = o_full[0:n_q, :]
    for h in range(1, n_heads):
        o = o + o_full[h * n_q:(h + 1) * n_q, :]
    o_ref[0] = o


def _attn_sample_call(q, k_new, v_new, cache_k, cache_v, layer):
    bsz, n_q, d_c = q.shape
    w_buf = cache_k.shape[2]
    new_spec = pl.BlockSpec((1, n_q, d_c), lambda b: (b, 0, 0))
    cache_spec = pl.BlockSpec((1, 1, w_buf, d_c), lambda b: (layer, b, 0, 0))
    return pl.pallas_call(
        functools.partial(_attn_sample_kernel, n_q=n_q, n_heads=d_c // HEAD_DIM, w_buf=w_buf),
        grid=(bsz,),
        in_specs=[new_spec, new_spec, new_spec, cache_spec, cache_spec],
        out_specs=new_spec,
        out_shape=jax.ShapeDtypeStruct((bsz, n_q, d_c), F32),
        compiler_params=pltpu.CompilerParams(
            dimension_semantics=("arbitrary",), vmem_limit_bytes=VMEM_LIMIT),
        name="attn_sample",
    )(q, k_new, v_new, cache_k, cache_v)


def _outffn_kernel(mixab_ref, o_ref, x_ref, mod_ref, goc_ref, wo_ref, gpm_ref, gpf_ref, wg_ref, wu_ref,
                   cfw_ref, wd_ref, gpo_ref, stf_ref,
                   y_ref, nf_ref, gbuf, *, rows, shift, carry_f, n_tiles):
    t = pl.program_id(1)

    @pl.when(t == 0)
    def _():
        gbuf[0:carry_f, :] = stf_ref[0]

    d_ab = mixab_ref.shape[2]
    oc = _rms(o_ref[0], goc_ref[...]).astype(BF16)
    y = _dot(mixab_ref[0], wo_ref[0:d_ab, :]) + _dot(oc, wo_ref[d_ab:, :])
    x1 = x_ref[0] + mod_ref[0, 2] * _rms(y, gpm_ref[...])

    h = (_rms(x1, gpf_ref[...]) * (1.0 + mod_ref[0, 4]) + mod_ref[0, 3]).astype(BF16)
    gbuf[carry_f:carry_f + rows, :] = _dot(h, wg_ref[...])
    hist = (FFN_CONV_WIDTH - 1) * shift
    g = _conv_rows(gbuf, cfw_ref, carry_f - hist, FFN_CONV_WIDTH, shift, rows, rows)
    f = (g * _sigmoid(g) * _dot(h, wu_ref[...])).astype(BF16)
    y2 = _dot(f, wd_ref[...])
    y_ref[0] = x1 + mod_ref[0, 5] * _rms(y2, gpo_ref[...])
    nf_ref[0] = gbuf[rows:rows + carry_f, :]
    if n_tiles > 1:
        gbuf[0:carry_f, :] = gbuf[rows:rows + carry_f, :]


def _outffn_call(mixab, o, x, mod, st_f, lw, *, rows, shift):
    groups, n, d = x.shape
    rm = mod.shape[2]
    d_ab = mixab.shape[2]
    d_c = o.shape[2]
    d_ff = lw['w_gate'].shape[1]
    carry_f = st_f.shape[1]
    n_tiles = n // rows
    kern = functools.partial(_outffn_kernel, rows=rows, shift=shift, carry_f=carry_f, n_tiles=n_tiles)
    tile = lambda w: pl.BlockSpec((1, rows, w), lambda g, t: (g, t, 0))
    per_group = lambda r, w: pl.BlockSpec((1, r, w), lambda g, t: (g, 0, 0))
    return pl.pallas_call(
        kern,
        grid=(groups, n_tiles),
        in_specs=[
            tile(d_ab), tile(d_c), tile(d),
            pl.BlockSpec((1, 6, rm, d), lambda g, t: (g, 0, 0, 0)),
            _const_spec((1, d_c)),
            _const_spec((d_ab + d_c, d)),
            _const_spec((1, d)),
            _const_spec((1, d)),
            _const_spec((d, d_ff)),
            _const_spec((d, d_ff)),
            _const_spec((FFN_CONV_WIDTH, d_ff)),
            _const_spec((d_ff, d)),
            _const_spec((1, d)),
            per_group(carry_f, d_ff),
        ],
        out_specs=[tile(d), per_group(carry_f, d_ff)],
        out_shape=[
            jax.ShapeDtypeStruct((groups, n, d), F32),
            jax.ShapeDtypeStruct((groups, carry_f, d_ff), F32),
        ],
        scratch_shapes=[pltpu.VMEM((carry_f + rows, d_ff), F32)],
        compiler_params=pltpu.CompilerParams(
            dimension_semantics=("arbitrary", "arbitrary"), vmem_limit_bytes=VMEM_LIMIT),
        name="out_ffn",
    )(mixab, o, x, mod, lw['g_out_c'], lw['w_o'], lw['g_post_mix'], lw['g_pre_ffn'], lw['w_gate'],
      lw['w_up'], lw['conv_f_w'], lw['w_down'], lw['g_post_ffn'], st_f)


def _front_pad(state, rows):
    return jnp.pad(state, ((0, 0), (rows - state.shape[1], 0), (0, 0)))


def _prompt_layer(x, mod, lw):
    bsz, seq, d = x.shape
    d_a, d_b = lw['conv_a_w'].shape[1], lw['conv_b_w'].shape[1]
    d_ff = lw['w_gate'].shape[1]
    rows = PROMPT_TILE_ROWS if seq % PROMPT_TILE_ROWS == 0 else seq
    carry_a = _round_up(CONV_A_WIDTH - 1, SUBLANES)
    carry_b = _round_up(CONV_B_WIDTH - 1, SUBLANES)
    carry_f = _round_up(FFN_CONV_WIDTH - 1, SUBLANES)
    mod4 = mod.reshape(bsz, 6, 1, d)
    mixab, q, k, v, na, nb = _mixin_call(
        x, mod4, jnp.zeros((bsz, carry_a, d_a), F32), jnp.zeros((bsz, carry_b, d_b), F32), lw,
        rows=rows, shift=1)
    o = _attn_call(q, k, v)
    y, nf = _outffn_call(mixab, o, x, mod4, jnp.zeros((bsz, carry_f, d_ff), F32), lw, rows=rows, shift=1)
    keep = min(DILATED_PATTERNS[-1][0], seq)
    n_heads = k.shape[2] // HEAD_DIM
    new_k = k[:, seq - keep:].reshape(bsz, keep, n_heads, HEAD_DIM)
    new_v = v[:, seq - keep:].reshape(bsz, keep, n_heads, HEAD_DIM)
    return (y, new_k, new_v, na[:, carry_a - (CONV_A_WIDTH - 1):], nb[:, carry_b - (CONV_B_WIDTH - 1):],
            nf[:, carry_f - (FFN_CONV_WIDTH - 1):])


def _time_major(state):
    bsz, k, c = state.shape
    return state.transpose(1, 0, 2).reshape(1, k * bsz, c)


def _batch_major(rows, bsz):
    _, n, c = rows.shape
    return rows.reshape(n // bsz, bsz, c).transpose(1, 0, 2)


def _sample_layer(x_tm, mod_tm, st_a, st_b, st_f, cache_k, cache_v, layer, lw, bsz):
    n = x_tm.shape[1]
    carry_a = _round_up((CONV_A_WIDTH - 1) * bsz, SUBLANES)
    carry_b = _round_up((CONV_B_WIDTH - 1) * bsz, SUBLANES)
    carry_f = _round_up((FFN_CONV_WIDTH - 1) * bsz, SUBLANES)
    mixab, q, k, v, na, nb = _mixin_call(
        x_tm, mod_tm, _front_pad(_time_major(st_a), carry_a), _front_pad(_time_major(st_b), carry_b), lw,
        rows=n, shift=bsz)
    qb, kb, vb = (_batch_major(a, bsz) for a in (q, k, v))
    o = _attn_sample_call(qb, kb, vb, cache_k, cache_v, layer)
    o_tm = o.transpose(1, 0, 2).reshape(1, n, o.shape[2])
    y, nf = _outffn_call(mixab, o_tm, x_tm, mod_tm, _front_pad(_time_major(st_f), carry_f), lw,
                         rows=n, shift=bsz)
    n_heads = kb.shape[2] // HEAD_DIM
    new_k = kb.reshape(bsz, kb.shape[1], n_heads, HEAD_DIM)
    new_v = vb.reshape(bsz, vb.shape[1], n_heads, HEAD_DIM)
    new_a = _batch_major(na[:, carry_a - (CONV_A_WIDTH - 1) * bsz:], bsz)
    new_b = _batch_major(nb[:, carry_b - (CONV_B_WIDTH - 1) * bsz:], bsz)
    new_f = _batch_major(nf[:, carry_f - (FFN_CONV_WIDTH - 1) * bsz:], bsz)
    return y, new_k, new_v, new_a, new_b, new_f


def kernel(x_prompt, x_sample, cache_k, cache_v, state_conv_a, state_conv_b, state_ffn_conv, c_prompt, c_sample, w_ada, b_ada, g_pre_mix, w_in, conv_a_w, conv_a_b, ln_a_g, ln_a_b, conv_b_w, g_out_a, g_out_b, g_out_c, w_o, g_post_mix, g_pre_ffn, w_gate, w_up, conv_f_w, w_down, g_post_ffn):
    depth = w_ada.shape[0]
    bsz_p, _, d = x_prompt.shape
    bsz_s, t_s, _ = x_sample.shape

    mod = _ada_call(jnp.concatenate([c_prompt, c_sample], axis=0), w_ada, b_ada)
    ck = cache_k.reshape(cache_k.shape[:3] + (-1,))
    cv = cache_v.reshape(cache_v.shape[:3] + (-1,))

    xp = x_prompt
    xs = x_sample.transpose(1, 0, 2).reshape(1, t_s * bsz_s, d)
    outs = [[] for _ in range(10)]
    for l in range(depth):
        row = lambda a: a[l][None, :]
        lw = {
            'g_pre_mix': row(g_pre_mix), 'w_in': w_in[l].astype(BF16),
            'conv_a_w': conv_a_w[l], 'conv_a_b': row(conv_a_b), 'ln_a_g': row(ln_a_g), 'ln_a_b': row(ln_a_b),
            'conv_b_w': conv_b_w[l], 'g_out_a': row(g_out_a), 'g_out_b': row(g_out_b), 'g_out_c': row(g_out_c),
            'w_o': w_o[l].astype(BF16), 'g_post_mix': row(g_post_mix), 'g_pre_ffn': row(g_pre_ffn),
            'w_gate': w_gate[l].astype(BF16), 'w_up': w_up[l].astype(BF16), 'conv_f_w': conv_f_w[l],
            'w_down': w_down[l].astype(BF16), 'g_post_ffn': row(g_post_ffn),
        }
        mod_p = mod[l, :bsz_p].reshape(bsz_p, 6, d)
        mod_s = mod[l, bsz_p:].reshape(bsz_s, 6, d).transpose(1, 0, 2)
        mod_s = jnp.tile(mod_s[:, None], (1, t_s, 1, 1)).reshape(1, 6, t_s * bsz_s, d)

        xp, kp, vp, ap, bp, fp = _prompt_layer(xp, mod_p, lw)
        xs, ks_, vs_, as_, bs_, fs_ = _sample_layer(
            xs, mod_s, state_conv_a[l], state_conv_b[l], state_ffn_conv[l], ck, cv, l, lw, bsz_s)
        for lst, val in zip(outs, (kp, vp, ks_, vs_, ap, as_, bp, bs_, fp, fs_)):
            lst.append(val)
    ys = xs.reshape(t_s, bsz_s, d).transpose(1, 0, 2)
    return (xp, ys) + tuple(jnp.stack(o) for o in outs)
```

```python
import functools

import jax
import jax.numpy as jnp
from jax import lax
from jax.experimental import pallas as pl
from jax.experimental.pallas import tpu as pltpu

F32 = jnp.float32
BF16 = jnp.bfloat16

HEAD_DIM = 64
LANES = 128
SUBLANES = 8
CONV_A_WIDTH = 31
CONV_B_WIDTH = 3
FFN_CONV_WIDTH = 3
DILATED_PATTERNS = ((128, 1), (512, 4), (2048, 16))
Q_BLK = 128
EPS = 1e-6
NEG = -1e30
LOG2_E = 1.4426950408889634
ATTN_UNROLL = 4
VMEM_LIMIT = 56 * 1024 * 1024
PROMPT_TILE_ROWS = 512


def _round_up(n, m):
    return (n + m - 1) // m * m


def _rms(x, g):
    return x * lax.rsqrt(jnp.mean(x * x, axis=-1, keepdims=True) + EPS) * g


def _sigmoid(x):
    return 1.0 / (1.0 + jnp.exp(-x))


def _dot(a, b):
    return jnp.dot(a, b, preferred_element_type=F32)


def _dot_nt(a, b):
    return lax.dot_general(a, b, (((1,), (1,)), ((), ())), preferred_element_type=F32)


def _const_spec(shape):
    zeros = (0,) * len(shape)
    return pl.BlockSpec(shape, lambda *_: zeros, pipeline_mode=pl.Buffered(1))


def _ada_kernel(c_ref, w_ref, b_ref, o_ref):
    c = c_ref[...]
    s = (c * _sigmoid(c)).astype(BF16)
    o_ref[0] = _dot(s, w_ref[0].astype(BF16)) + b_ref[0]


def _ada_call(c_all, w_ada, b_ada):
    depth, d, n = w_ada.shape
    rows = c_all.shape[0]
    tn = n // 4
    return pl.pallas_call(
        _ada_kernel,
        grid=(depth, n // tn),
        in_specs=[
            pl.BlockSpec((rows, d), lambda l, j: (0, 0)),
            pl.BlockSpec((1, d, tn), lambda l, j: (l, 0, j)),
            pl.BlockSpec((1, 1, tn), lambda l, j: (l, 0, j)),
        ],
        out_specs=pl.BlockSpec((1, rows, tn), lambda l, j: (l, 0, j)),
        out_shape=jax.ShapeDtypeStruct((depth, rows, n), F32),
        compiler_params=pltpu.CompilerParams(
            dimension_semantics=("arbitrary", "arbitrary"), vmem_limit_bytes=VMEM_LIMIT),
        name="ada_mod",
    )(c_all, w_ada, b_ada.reshape(depth, 1, n))


def _conv_rows(buf_ref, w_ref, base, ntaps, shift, rows, chunk):
    by_residue = {}
    for j in range(ntaps):
        off = base + j * shift
        by_residue.setdefault(off % SUBLANES, []).append((j, off - off % SUBLANES))
    outs = []
    for c0 in range(0, rows, chunk):
        acc = None
        for res, taps in sorted(by_residue.items()):
            ext = chunk + (SUBLANES if res else 0)
            part = None
            for j, off in taps:
                term = buf_ref[pl.ds(c0 + off, ext), :] * w_ref[j:j + 1, :]
                part = term if part is None else part + term
            if res:
                part = part[res:res + chunk]
            acc = part if acc is None else acc + part
        outs.append(acc)
    return outs[0] if len(outs) == 1 else jnp.concatenate(outs, axis=0)


def _mixin_kernel(x_ref, mod_ref, gpre_ref, win_ref, caw_ref, cab_ref, lng_ref, lnb_ref, cbw_ref,
                  goa_ref, gob_ref, sta_ref, stb_ref,
                  mixab_ref, q_ref, k_ref, v_ref, na_ref, nb_ref, *rest,
                  rows, shift, d_a, d_b, d_c, carry_a, carry_b, n_tiles, first_kept_tile):
    abuf, ubuf = rest[-2:]
    t = pl.program_id(1)

    @pl.when(t == 0)
    def _():
        abuf[0:carry_a, :] = sta_ref[0]
        ubuf[0:carry_b, :] = stb_ref[0]

    x = x_ref[0]
    h = (_rms(x, gpre_ref[...]) * (1.0 + mod_ref[0, 1]) + mod_ref[0, 0]).astype(BF16)

    za = _dot(h, win_ref[:, 0:2 * d_a])
    abuf[carry_a:carry_a + rows, :] = za[:, 0:d_a] * _sigmoid(za[:, d_a:2 * d_a])
    hist_a = (CONV_A_WIDTH - 1) * shift
    a = _conv_rows(abuf, caw_ref, carry_a - hist_a, CONV_A_WIDTH, shift, rows, min(rows, 64)) + cab_ref[...]
    mu = jnp.mean(a, axis=-1, keepdims=True)
    ac = a - mu
    var = jnp.mean(ac * ac, axis=-1, keepdims=True)
    a = ac * lax.rsqrt(var + EPS) * lng_ref[...] + lnb_ref[...]
    a = a * _sigmoid(a)
    mixab_ref[0, :, 0:d_a] = _rms(a, goa_ref[...]).astype(BF16)
    na_ref[0] = abuf[rows:rows + carry_a, :]

    o_b = 2 * d_a
    zb = _dot(h, win_ref[:, o_b:o_b + 3 * d_b])
    ubuf[carry_b:carry_b + rows, :] = zb[:, 2 * d_b:3 * d_b] * zb[:, 0:d_b]
    hist_b = (CONV_B_WIDTH - 1) * shift
    u = _conv_rows(ubuf, cbw_ref, carry_b - hist_b, CONV_B_WIDTH, shift, rows, rows)
    bo = zb[:, d_b:2 * d_b] * u
    mixab_ref[0, :, d_a:d_a + d_b] = _rms(bo, gob_ref[...]).astype(BF16)
    nb_ref[0] = ubuf[rows:rows + carry_b, :]

    o_c = o_b + 3 * d_b
    q_ref[0] = _dot(h, win_ref[:, o_c:o_c + d_c])
    k = _dot(h, win_ref[:, o_c + d_c:o_c + 2 * d_c])
    v = _dot(h, win_ref[:, o_c + 2 * d_c:o_c + 3 * d_c])
    k_ref[0] = k
    v_ref[0] = v
    if first_kept_tile is not None:
        kept_k_ref, kept_v_ref = rest[:2]

        @pl.when(t >= first_kept_tile)
        def _():
            kept_k_ref[0] = k
            kept_v_ref[0] = v

    if n_tiles > 1:
        abuf[0:carry_a, :] = abuf[rows:rows + carry_a, :]
        ubuf[0:carry_b, :] = ubuf[rows:rows + carry_b, :]


def _mixin_call(x, mod, st_a, st_b, lw, *, rows, shift, keep):
    groups, n, d = x.shape
    rm = mod.shape[2]
    d_a = lw['conv_a_w'].shape[1]
    d_b = lw['conv_b_w'].shape[1]
    d_c = lw['g_out_c'].shape[1]
    p_in = lw['w_in'].shape[1]
    carry_a, carry_b = st_a.shape[1], st_b.shape[1]
    n_tiles = n // rows
    assert keep % rows == 0 and keep <= n
    first_kept_tile = (n - keep) // rows if keep < n else None
    kern = functools.partial(_mixin_kernel, rows=rows, shift=shift, d_a=d_a, d_b=d_b, d_c=d_c,
                             carry_a=carry_a, carry_b=carry_b, n_tiles=n_tiles, first_kept_tile=first_kept_tile)
    kept_specs, kept_shapes = [], []
    if first_kept_tile is not None:
        kept_spec = pl.BlockSpec((1, rows, d_c), lambda g, t: (g, jnp.maximum(t - first_kept_tile, 0), 0))
        kept_specs = [kept_spec, kept_spec]
        kept_shapes = [jax.ShapeDtypeStruct((groups, keep, d_c), F32)] * 2
    tile = lambda w: pl.BlockSpec((1, rows, w), lambda g, t: (g, t, 0))
    per_group = lambda r, w: pl.BlockSpec((1, r, w), lambda g, t: (g, 0, 0))
    outs = pl.pallas_call(
        kern,
        grid=(groups, n_tiles),
        in_specs=[
            tile(d),
            pl.BlockSpec((1, 6, rm, d), lambda g, t: (g, 0, 0, 0)),
            _const_spec((1, d)),
            _const_spec((d, p_in)),
            _const_spec((CONV_A_WIDTH, d_a)),
            _const_spec((1, d_a)),
            _const_spec((1, d_a)),
            _const_spec((1, d_a)),
            _const_spec((CONV_B_WIDTH, d_b)),
            _const_spec((1, d_a)),
            _const_spec((1, d_b)),
            per_group(carry_a, d_a),
            per_group(carry_b, d_b),
        ],
        out_specs=[
            tile(d_a + d_b), tile(d_c), tile(d_c), tile(d_c),
            per_group(carry_a, d_a), per_group(carry_b, d_b),
        ] + kept_specs,
        out_shape=[
            jax.ShapeDtypeStruct((groups, n, d_a + d_b), BF16),
            jax.ShapeDtypeStruct((groups, n, d_c), F32),
            jax.ShapeDtypeStruct((groups, n, d_c), F32),
            jax.ShapeDtypeStruct((groups, n, d_c), F32),
            jax.ShapeDtypeStruct((groups, carry_a, d_a), F32),
            jax.ShapeDtypeStruct((groups, carry_b, d_b), F32),
        ] + kept_shapes,
        scratch_shapes=[
            pltpu.VMEM((carry_a + rows, d_a), F32),
            pltpu.VMEM((carry_b + rows, d_b), F32),
        ],
        compiler_params=pltpu.CompilerParams(
            dimension_semantics=("arbitrary", "arbitrary"), vmem_limit_bytes=VMEM_LIMIT),
        name="mix_in",
    )(x, mod, lw['g_pre_mix'], lw['w_in'], lw['conv_a_w'], lw['conv_a_b'], lw['ln_a_g'], lw['ln_a_b'],
      lw['conv_b_w'], lw['g_out_a'], lw['g_out_b'], st_a, st_b)
    if first_kept_tile is None:
        outs = list(outs) + [outs[2], outs[3]]
    return outs


def _attn_kernel(q_ref, k_ref, v_ref, o_ref, qs0, qs1, kts, vs, acc, mrun, lrun, bias, pbuf, mbuf, *, seq):
    n_blk = seq // Q_BLK
    lane = lax.broadcasted_iota(jnp.int32, (1, LANES), 1)
    head0 = lane < HEAD_DIM
    scale = HEAD_DIM ** -0.5 * LOG2_E

    row = lax.broadcasted_iota(jnp.int32, (Q_BLK, 2 * Q_BLK), 0)
    col = lax.broadcasted_iota(jnp.int32, (Q_BLK, 2 * Q_BLK), 1)
    band = jnp.logical_and(col >= row, col <= row + Q_BLK)
    bias[0] = jnp.where(band, 0.0, NEG)
    bias[1] = jnp.where(jnp.logical_and(band, col >= Q_BLK), 0.0, NEG)
    vs[0:Q_BLK, 0:LANES] = jnp.zeros((Q_BLK, LANES), BF16)
    vs[:, LANES:2 * LANES] = jnp.ones((seq + Q_BLK, LANES), BF16)
    kts[0] = jnp.zeros((LANES, Q_BLK), BF16)

    n_pat = len(DILATED_PATTERNS)
    for pat, (_, dil) in enumerate(DILATED_PATTERNS):
        blk_per_phase = n_blk // dil

        def rows_of(j, dil=dil, blk_per_phase=blk_per_phase):
            if dil == 1:
                return pl.ds(pl.multiple_of(j * Q_BLK, Q_BLK), Q_BLK)
            phase = j // blk_per_phase
            i = j - phase * blk_per_phase
            return pl.ds(phase + i * (Q_BLK * dil), Q_BLK, stride=dil)

        def split(j, carry, rows_of=rows_of):
            src = rows_of(j)
            dst = pl.ds(pl.multiple_of(j * Q_BLK, Q_BLK), Q_BLK)
            dst_kv = pl.ds(pl.multiple_of((j + 1) * Q_BLK, Q_BLK), Q_BLK)
            qv = q_ref.at[0][src, :] * scale
            qs0[dst, :] = jnp.where(head0, qv, 0.0).astype(BF16)
            qs1[dst, :] = jnp.where(head0, 0.0, qv).astype(BF16)
            kts[j + 1] = k_ref.at[0][src, :].T.astype(BF16)
            vs[dst_kv, 0:LANES] = v_ref.at[0][src, :].astype(BF16)
            return carry

        lax.fori_loop(0, n_blk, split, 0, unroll=ATTN_UNROLL)

        def scores(g, slot, blk_per_phase=blk_per_phase):
            for u in range(ATTN_UNROLL):
                j = g * ATTN_UNROLL + u
                qrows = pl.ds(pl.multiple_of(j * Q_BLK, Q_BLK), Q_BLK)
                kbt = jnp.concatenate([kts[j], kts[j + 1]], axis=1)
                bb = bias[(jnp.asarray(j, jnp.int32) % blk_per_phase == 0).astype(jnp.int32)]
                ms = []
                for h, qs in enumerate((qs0, qs1)):
                    s = _dot(qs[qrows, :], kbt) + bb
                    m = jnp.max(s, axis=-1, keepdims=True)
                    pbuf[slot, u, h] = jnp.exp2(s - m).astype(BF16)
                    ms.append(m)
                mbuf[slot, u] = jnp.where(head0, ms[0], ms[1])

        def values(g, slot, pat=pat, rows_of=rows_of):
            for u in range(ATTN_UNROLL):
                j = g * ATTN_UNROLL + u
                vb = vs[pl.ds(pl.multiple_of(j * Q_BLK, Q_BLK), 2 * Q_BLK), :]
                pv0 = _dot(pbuf[slot, u, 0], vb)
                pv1 = _dot(pbuf[slot, u, 1], vb)
                a_new = jnp.where(head0, pv0[:, :LANES], pv1[:, :LANES])
                l_new = jnp.where(head0, pv0[:, LANES:], pv1[:, LANES:])
                m_new = mbuf[slot, u]
                dst = rows_of(j)
                if pat > 0:
                    m_old = mrun[dst, :]
                    m_tot = jnp.maximum(m_old, m_new)
                    w_old = jnp.exp2(m_old - m_tot)
                    w_new = jnp.exp2(m_new - m_tot)
                    a_new = w_old * acc[dst, :] + w_new * a_new
                    l_new = w_old * lrun[dst, :] + w_new * l_new
                    m_new = m_tot
                if pat < n_pat - 1:
                    mrun[dst, :] = m_new
                    lrun[dst, :] = l_new
                    acc[dst, :] = a_new
                else:
                    o_ref.at[0][dst, :] = a_new / l_new

        n_grp = n_blk // ATTN_UNROLL
        scores(0, 0)

        def pair(t, carry, scores=scores, values=values, n_grp=n_grp):
            g = 2 * t
            scores(g + 1, 1)
            values(g, 0)
            scores(jnp.minimum(g + 2, n_grp - 1), 0)
            values(g + 1, 1)
            return carry

        lax.fori_loop(0, n_grp // 2, pair, 0)


def _attn_call(q, k, v):
    bsz, seq, d_c = q.shape
    assert seq % (Q_BLK * DILATED_PATTERNS[-1][1]) == 0 and seq % (2 * ATTN_UNROLL * Q_BLK) == 0
    spec = pl.BlockSpec((1, seq, LANES), lambda b, h: (b, 0, h))
    return pl.pallas_call(
        functools.partial(_attn_kernel, seq=seq),
        grid=(bsz, d_c // LANES),
        in_specs=[spec, spec, spec],
        out_specs=spec,
        out_shape=jax.ShapeDtypeStruct((bsz, seq, d_c), F32),
        scratch_shapes=[
            pltpu.VMEM((seq, LANES), BF16),
            pltpu.VMEM((seq, LANES), BF16),
            pltpu.VMEM((seq // Q_BLK + 1, LANES, Q_BLK), BF16),
            pltpu.VMEM((seq + Q_BLK, 2 * LANES), BF16),
            pltpu.VMEM((seq, LANES), F32),
            pltpu.VMEM((seq, LANES), F32),
            pltpu.VMEM((seq, LANES), F32),
            pltpu.VMEM((2, Q_BLK, 2 * Q_BLK), F32),
            pltpu.VMEM((2, ATTN_UNROLL, 2, Q_BLK, 2 * Q_BLK), BF16),
            pltpu.VMEM((2, ATTN_UNROLL, Q_BLK, LANES), F32),
        ],
        compiler_params=pltpu.CompilerParams(
            dimension_semantics=("arbitrary", "arbitrary"), vmem_limit_bytes=VMEM_LIMIT),
        name="attn_prompt",
    )(q, k, v)


def _multiplicity(dist):
    cnt = jnp.zeros(dist.shape, F32)
    for window, dil in DILATED_PATTERNS:
        hit = jnp.logical_and(dist >= 0, jnp.logical_and(dist <= window, dist % dil == 0))
        cnt = cnt + jnp.where(hit, 1.0, 0.0)
    return cnt


def _attn_sample_kernel(q_ref, kn_ref, vn_ref, kfar_ref, krec_ref, vfar_ref, vrec_ref, kidx_ref, o_ref, *,
                        n_q, n_heads, w_buf):
    d_c = n_heads * HEAD_DIM
    n_rows = n_heads * n_q
    scale = HEAD_DIM ** -0.5
    q = q_ref[0] * scale
    q_rep = jnp.concatenate([q] * n_heads, axis=0)
    row_head = lax.broadcasted_iota(jnp.int32, (n_rows, d_c), 0) // n_q
    lane_head = lax.broadcasted_iota(jnp.int32, (n_rows, d_c), 1) // HEAD_DIM
    own = row_head == lane_head
    q_bd = jnp.where(own, q_rep, 0.0).astype(BF16)

    def q_index(shape):
        r = lax.broadcasted_iota(jnp.int32, shape, 0)
        return w_buf + r - (r // n_q) * n_q

    k_c = jnp.concatenate([kfar_ref[0, 0], krec_ref[0, 0]], axis=0).astype(BF16)
    v_c = jnp.concatenate([vfar_ref[0, 0], vrec_ref[0, 0]], axis=0).astype(BF16)
    n_c = k_c.shape[0]
    s_c = _dot_nt(q_bd, k_c)
    cnt_c = _multiplicity(q_index((n_rows, n_c)) - kidx_ref[...])
    s_c = jnp.where(cnt_c > 0.0, s_c, NEG)
    s_n = _dot_nt(q_bd, kn_ref[0].astype(BF16))
    key_n = w_buf + lax.broadcasted_iota(jnp.int32, (n_rows, n_q), 1)
    cnt_n = _multiplicity(q_index((n_rows, n_q)) - key_n)
    s_n = jnp.where(cnt_n > 0.0, s_n, NEG)

    m = jnp.maximum(jnp.max(s_c, axis=-1, keepdims=True), jnp.max(s_n, axis=-1, keepdims=True))
    p_c = cnt_c * jnp.exp(s_c - m)
    p_n = cnt_n * jnp.exp(s_n - m)
    l = jnp.sum(p_c, axis=-1, keepdims=True) + jnp.sum(p_n, axis=-1, keepdims=True)
    o_full = _dot(p_c.astype(BF16), v_c) + _dot(p_n.astype(BF16), vn_ref[0].astype(BF16))
    o_full = jnp.where(own, o_full / l, 0.0)
    o = o_full[0:n_q, :]
    for h in range(1, n_heads):
        o = o + o_full[h * n_q:(h + 1) * n_q, :]
    o_ref[0] = o


def _select_cache_rows(cache, n_q):
    depth, bsz, w_buf, n_heads, head_dim = cache.shape
    d_c = n_heads * head_dim
    dil = DILATED_PATTERNS[-1][1]
    near = max(w for w, _ in DILATED_PATTERNS[:-1])
    assert w_buf % dil == 0 and near % dil == 0 and near < w_buf and n_q <= dil
    n_far = (w_buf - near) // dil
    far = cache.reshape(depth, bsz, w_buf // dil, dil, n_heads, head_dim)[:, :, :n_far, :n_q]
    far = far.reshape(depth, bsz, n_far * n_q, d_c)
    recent = cache[:, :, w_buf - near:].reshape(depth, bsz, near, d_c)
    far_pos = (jnp.arange(n_far, dtype=jnp.int32)[:, None] * dil + jnp.arange(n_q, dtype=jnp.int32)[None, :])
    pos = jnp.concatenate([far_pos.reshape(-1), jnp.arange(w_buf - near, w_buf, dtype=jnp.int32)])
    return far, recent, pos[None, :]


def _attn_sample_call(q, k_new, v_new, k_sel, v_sel, layer, w_buf):
    bsz, n_q, d_c = q.shape
    k_far, k_rec, kidx = k_sel
    v_far, v_rec, _ = v_sel
    new_spec = pl.BlockSpec((1, n_q, d_c), lambda b: (b, 0, 0))
    far_spec = pl.BlockSpec((1, 1, k_far.shape[2], d_c), lambda b: (layer, b, 0, 0))
    rec_spec = pl.BlockSpec((1, 1, k_rec.shape[2], d_c), lambda b: (layer, b, 0, 0))
    return pl.pallas_call(
        functools.partial(_attn_sample_kernel, n_q=n_q, n_heads=d_c // HEAD_DIM, w_buf=w_buf),
        grid=(bsz,),
        in_specs=[new_spec, new_spec, new_spec, far_spec, rec_spec, far_spec, rec_spec,
                  pl.BlockSpec(kidx.shape, lambda b: (0, 0))],
        out_specs=new_spec,
        out_shape=jax.ShapeDtypeStruct((bsz, n_q, d_c), F32),
        compiler_params=pltpu.CompilerParams(
            dimension_semantics=("arbitrary",), vmem_limit_bytes=VMEM_LIMIT),
        name="attn_sample",
    )(q, k_new, v_new, k_far, k_rec, v_far, v_rec, kidx)


def _outffn_kernel(mixab_ref, o_ref, x_ref, mod_ref, goc_ref, wo_ref, gpm_ref, gpf_ref, wg_ref, wu_ref,
                   cfw_ref, wd_ref, gpo_ref, stf_ref,
                   y_ref, nf_ref, gbuf, *, rows, shift, carry_f, n_tiles):
    t = pl.program_id(1)

    @pl.when(t == 0)
    def _():
        gbuf[0:carry_f, :] = stf_ref[0]

    d_ab = mixab_ref.shape[2]
    oc = _rms(o_ref[0], goc_ref[...]).astype(BF16)
    y = _dot(mixab_ref[0], wo_ref[0:d_ab, :]) + _dot(oc, wo_ref[d_ab:, :])
    x1 = x_ref[0] + mod_ref[0, 2] * _rms(y, gpm_ref[...])

    h = (_rms(x1, gpf_ref[...]) * (1.0 + mod_ref[0, 4]) + mod_ref[0, 3]).astype(BF16)
    gbuf[carry_f:carry_f + rows, :] = _dot(h, wg_ref[...])
    hist = (FFN_CONV_WIDTH - 1) * shift
    g = _conv_rows(gbuf, cfw_ref, carry_f - hist, FFN_CONV_WIDTH, shift, rows, rows)
    f = (g * _sigmoid(g) * _dot(h, wu_ref[...])).astype(BF16)
    y2 = _dot(f, wd_ref[...])
    y_ref[0] = x1 + mod_ref[0, 5] * _rms(y2, gpo_ref[...])
    nf_ref[0] = gbuf[rows:rows + carry_f, :]
    if n_tiles > 1:
        gbuf[0:carry_f, :] = gbuf[rows:rows + carry_f, :]


def _outffn_call(mixab, o, x, mod, st_f, lw, *, rows, shift):
    groups, n, d = x.shape
    rm = mod.shape[2]
    d_ab = mixab.shape[2]
    d_c = o.shape[2]
    d_ff = lw['w_gate'].shape[1]
    carry_f = st_f.shape[1]
    n_tiles = n // rows
    kern = functools.partial(_outffn_kernel, rows=rows, shift=shift, carry_f=carry_f, n_tiles=n_tiles)
    tile = lambda w: pl.BlockSpec((1, rows, w), lambda g, t: (g, t, 0))
    per_group = lambda r, w: pl.BlockSpec((1, r, w), lambda g, t: (g, 0, 0))
    return pl.pallas_call(
        kern,
        grid=(groups, n_tiles),
        in_specs=[
            tile(d_ab), tile(d_c), tile(d),
            pl.BlockSpec((1, 6, rm, d), lambda g, t: (g, 0, 0, 0)),
            _const_spec((1, d_c)),
            _const_spec((d_ab + d_c, d)),
            _const_spec((1, d)),
            _const_spec((1, d)),
            _const_spec((d, d_ff)),
            _const_spec((d, d_ff)),
            _const_spec((FFN_CONV_WIDTH, d_ff)),
            _const_spec((d_ff, d)),
            _const_spec((1, d)),
            per_group(carry_f, d_ff),
        ],
        out_specs=[tile(d), per_group(carry_f, d_ff)],
        out_shape=[
            jax.ShapeDtypeStruct((groups, n, d), F32),
            jax.ShapeDtypeStruct((groups, carry_f, d_ff), F32),
        ],
        scratch_shapes=[pltpu.VMEM((carry_f + rows, d_ff), F32)],
        compiler_params=pltpu.CompilerParams(
            dimension_semantics=("arbitrary", "arbitrary"), vmem_limit_bytes=VMEM_LIMIT),
        name="out_ffn",
    )(mixab, o, x, mod, lw['g_out_c'], lw['w_o'], lw['g_post_mix'], lw['g_pre_ffn'], lw['w_gate'],
      lw['w_up'], lw['conv_f_w'], lw['w_down'], lw['g_post_ffn'], st_f)


def _front_pad(state, rows):
    return jnp.pad(state, ((0, 0), (rows - state.shape[1], 0), (0, 0)))


def _prompt_layer(x, mod, lw):
    bsz, seq, d = x.shape
    d_a, d_b = lw['conv_a_w'].shape[1], lw['conv_b_w'].shape[1]
    d_ff = lw['w_gate'].shape[1]
    rows = PROMPT_TILE_ROWS if seq % PROMPT_TILE_ROWS == 0 else seq
    carry_a = _round_up(CONV_A_WIDTH - 1, SUBLANES)
    carry_b = _round_up(CONV_B_WIDTH - 1, SUBLANES)
    carry_f = _round_up(FFN_CONV_WIDTH - 1, SUBLANES)
    mod4 = mod.reshape(bsz, 6, 1, d)
    keep = min(DILATED_PATTERNS[-1][0], seq)
    mixab, q, k, v, na, nb, kept_k, kept_v = _mixin_call(
        x, mod4, jnp.zeros((bsz, carry_a, d_a), F32), jnp.zeros((bsz, carry_b, d_b), F32), lw,
        rows=rows, shift=1, keep=keep)
    o = _attn_call(q, k, v)
    y, nf = _outffn_call(mixab, o, x, mod4, jnp.zeros((bsz, carry_f, d_ff), F32), lw, rows=rows, shift=1)
    n_heads = k.shape[2] // HEAD_DIM
    new_k = kept_k.reshape(bsz, keep, n_heads, HEAD_DIM)
    new_v = kept_v.reshape(bsz, keep, n_heads, HEAD_DIM)
    return (y, new_k, new_v, na[:, carry_a - (CONV_A_WIDTH - 1):], nb[:, carry_b - (CONV_B_WIDTH - 1):],
            nf[:, carry_f - (FFN_CONV_WIDTH - 1):])


def _time_major(state):
    bsz, k, c = state.shape
    return state.transpose(1, 0, 2).reshape(1, k * bsz, c)


def _batch_major(rows, bsz):
    _, n, c = rows.shape
    return rows.reshape(n // bsz, bsz, c).transpose(1, 0, 2)


def _sample_layer(x_tm, mod_tm, st_a, st_b, st_f, k_sel, v_sel, w_buf, layer, lw, bsz):
    n = x_tm.shape[1]
    carry_a = _round_up((CONV_A_WIDTH - 1) * bsz, SUBLANES)
    carry_b = _round_up((CONV_B_WIDTH - 1) * bsz, SUBLANES)
    carry_f = _round_up((FFN_CONV_WIDTH - 1) * bsz, SUBLANES)
    mixab, q, k, v, na, nb, _, _ = _mixin_call(
        x_tm, mod_tm, _front_pad(_time_major(st_a), carry_a), _front_pad(_time_major(st_b), carry_b), lw,
        rows=n, shift=bsz, keep=n)
    qb, kb, vb = (_batch_major(a, bsz) for a in (q, k, v))
    o = _attn_sample_call(qb, kb, vb, k_sel, v_sel, layer, w_buf)
    o_tm = o.transpose(1, 0, 2).reshape(1, n, o.shape[2])
    y, nf = _outffn_call(mixab, o_tm, x_tm, mod_tm, _front_pad(_time_major(st_f), carry_f), lw,
                         rows=n, shift=bsz)
    n_heads = kb.shape[2] // HEAD_DIM
    new_k = kb.reshape(bsz, kb.shape[1], n_heads, HEAD_DIM)
    new_v = vb.reshape(bsz, vb.shape[1], n_heads, HEAD_DIM)
    new_a = _batch_major(na[:, carry_a - (CONV_A_WIDTH - 1) * bsz:], bsz)
    new_b = _batch_major(nb[:, carry_b - (CONV_B_WIDTH - 1) * bsz:], bsz)
    new_f = _batch_major(nf[:, carry_f - (FFN_CONV_WIDTH - 1) * bsz:], bsz)
    return y, new_k, new_v, new_a, new_b, new_f


def kernel(x_prompt, x_sample, cache_k, cache_v, state_conv_a, state_conv_b, state_ffn_conv, c_prompt, c_sample, w_ada, b_ada, g_pre_mix, w_in, conv_a_w, conv_a_b, ln_a_g, ln_a_b, conv_b_w, g_out_a, g_out_b, g_out_c, w_o, g_post_mix, g_pre_ffn, w_gate, w_up, conv_f_w, w_down, g_post_ffn):
    depth = w_ada.shape[0]
    bsz_p, _, d = x_prompt.shape
    bsz_s, t_s, _ = x_sample.shape

    mod = _ada_call(jnp.concatenate([c_prompt, c_sample], axis=0), w_ada, b_ada)
    k_sel = _select_cache_rows(cache_k, t_s)
    v_sel = _select_cache_rows(cache_v, t_s)

    xp = x_prompt
    xs = x_sample.transpose(1, 0, 2).reshape(1, t_s * bsz_s, d)
    outs = [[] for _ in range(10)]
    for l in range(depth):
        row = lambda a: a[l][None, :]
        lw = {
            'g_pre_mix': row(g_pre_mix), 'w_in': w_in[l].astype(BF16),
            'conv_a_w': conv_a_w[l], 'conv_a_b': row(conv_a_b), 'ln_a_g': row(ln_a_g), 'ln_a_b': row(ln_a_b),
            'conv_b_w': conv_b_w[l], 'g_out_a': row(g_out_a), 'g_out_b': row(g_out_b), 'g_out_c': row(g_out_c),
            'w_o': w_o[l].astype(BF16), 'g_post_mix': row(g_post_mix), 'g_pre_ffn': row(g_pre_ffn),
            'w_gate': w_gate[l].astype(BF16), 'w_up': w_up[l].astype(BF16), 'conv_f_w': conv_f_w[l],
            'w_down': w_down[l].astype(BF16), 'g_post_ffn': row(g_post_ffn),
        }
        mod_p = mod[l, :bsz_p].reshape(bsz_p, 6, d)
        mod_s = mod[l, bsz_p:].reshape(bsz_s, 6, d).transpose(1, 0, 2)
        mod_s = jnp.tile(mod_s[:, None], (1, t_s, 1, 1)).reshape(1, 6, t_s * bsz_s, d)

        xp, kp, vp, ap, bp, fp = _prompt_layer(xp, mod_p, lw)
        xs, ks_, vs_, as_, bs_, fs_ = _sample_layer(
            xs, mod_s, state_conv_a[l], state_conv_b[l], state_ffn_conv[l], k_sel, v_sel, cache_k.shape[2],
            l, lw, bsz_s)
        for lst, val in zip(outs, (kp, vp, ks_, vs_, ap, as_, bp, bs_, fp, fs_)):
            lst.append(val)
    ys = xs.reshape(t_s, bsz_s, d).transpose(1, 0, 2)
    return (xp, ys) + tuple(jnp.stack(o) for o in outs)
```

```python
import functools

import jax
import jax.numpy as jnp
from jax import lax
from jax.experimental import pallas as pl
from jax.experimental.pallas import tpu as pltpu

F32 = jnp.float32
BF16 = jnp.bfloat16

HEAD_DIM = 64
LANES = 128
SUBLANES = 8
CONV_A_WIDTH = 31
CONV_B_WIDTH = 3
FFN_CONV_WIDTH = 3
DILATED_PATTERNS = ((128, 1), (512, 4), (2048, 16))
Q_BLK = 128
EPS = 1e-6
NEG = -1e30
LOG2_E = 1.4426950408889634
ATTN_UNROLL = 4
VMEM_LIMIT = 56 * 1024 * 1024
PROMPT_TILE_ROWS = 512


def _round_up(n, m):
    return (n + m - 1) // m * m


def _rms(x, g):
    return x * lax.rsqrt(jnp.mean(x * x, axis=-1, keepdims=True) + EPS) * g


def _sigmoid(x):
    return 1.0 / (1.0 + jnp.exp(-x))


def _dot(a, b):
    return jnp.dot(a, b, preferred_element_type=F32)


def _dot_nt(a, b):
    return lax.dot_general(a, b, (((1,), (1,)), ((), ())), preferred_element_type=F32)


def _const_spec(shape):
    zeros = (0,) * len(shape)
    return pl.BlockSpec(shape, lambda *_: zeros, pipeline_mode=pl.Buffered(1))


def _ada_kernel(c_ref, w_ref, b_ref, o_ref):
    c = c_ref[...]
    s = (c * _sigmoid(c)).astype(BF16)
    o_ref[0] = _dot(s, w_ref[0].astype(BF16)) + b_ref[0]


def _ada_call(c_all, w_ada, b_ada):
    depth, d, n = w_ada.shape
    rows = c_all.shape[0]
    tn = n // 4
    return pl.pallas_call(
        _ada_kernel,
        grid=(depth, n // tn),
        in_specs=[
            pl.BlockSpec((rows, d), lambda l, j: (0, 0)),
            pl.BlockSpec((1, d, tn), lambda l, j: (l, 0, j)),
            pl.BlockSpec((1, 1, tn), lambda l, j: (l, 0, j)),
        ],
        out_specs=pl.BlockSpec((1, rows, tn), lambda l, j: (l, 0, j)),
        out_shape=jax.ShapeDtypeStruct((depth, rows, n), F32),
        compiler_params=pltpu.CompilerParams(
            dimension_semantics=("arbitrary", "arbitrary"), vmem_limit_bytes=VMEM_LIMIT),
        name="ada_mod",
    )(c_all, w_ada, b_ada.reshape(depth, 1, n))


def _conv_rows(buf_ref, w_ref, base, ntaps, shift, rows, chunk):
    by_residue = {}
    for j in range(ntaps):
        off = base + j * shift
        by_residue.setdefault(off % SUBLANES, []).append((j, off - off % SUBLANES))
    outs = []
    for c0 in range(0, rows, chunk):
        acc = None
        for res, taps in sorted(by_residue.items()):
            ext = chunk + (SUBLANES if res else 0)
            part = None
            for j, off in taps:
                term = buf_ref[pl.ds(c0 + off, ext), :] * w_ref[j:j + 1, :]
                part = term if part is None else part + term
            if res:
                part = part[res:res + chunk]
            acc = part if acc is None else acc + part
        outs.append(acc)
    return outs[0] if len(outs) == 1 else jnp.concatenate(outs, axis=0)


def _mixin_kernel(x_ref, mod_ref, gpre_ref, win_ref, caw_ref, cab_ref, lng_ref, lnb_ref, cbw_ref,
                  goa_ref, gob_ref, sta_ref, stb_ref,
                  mixab_ref, q_ref, k_ref, v_ref, na_ref, nb_ref, *rest,
                  rows, shift, d_a, d_b, d_c, carry_a, carry_b, n_tiles, first_kept_tile):
    abuf, ubuf = rest[-2:]
    t = pl.program_id(1)

    @pl.when(t == 0)
    def _():
        abuf[0:carry_a, :] = sta_ref[0]
        ubuf[0:carry_b, :] = stb_ref[0]

    x = x_ref[0]
    h = (_rms(x, gpre_ref[...]) * (1.0 + mod_ref[0, 1]) + mod_ref[0, 0]).astype(BF16)

    za = _dot(h, win_ref[:, 0:2 * d_a])
    abuf[carry_a:carry_a + rows, :] = za[:, 0:d_a] * _sigmoid(za[:, d_a:2 * d_a])
    hist_a = (CONV_A_WIDTH - 1) * shift
    a = _conv_rows(abuf, caw_ref, carry_a - hist_a, CONV_A_WIDTH, shift, rows, min(rows, 64)) + cab_ref[...]
    mu = jnp.mean(a, axis=-1, keepdims=True)
    ac = a - mu
    var = jnp.mean(ac * ac, axis=-1, keepdims=True)
    a = ac * lax.rsqrt(var + EPS) * lng_ref[...] + lnb_ref[...]
    a = a * _sigmoid(a)
    mixab_ref[0, :, 0:d_a] = _rms(a, goa_ref[...]).astype(BF16)
    na_ref[0] = abuf[rows:rows + carry_a, :]

    o_b = 2 * d_a
    zb = _dot(h, win_ref[:, o_b:o_b + 3 * d_b])
    ubuf[carry_b:carry_b + rows, :] = zb[:, 2 * d_b:3 * d_b] * zb[:, 0:d_b]
    hist_b = (CONV_B_WIDTH - 1) * shift
    u = _conv_rows(ubuf, cbw_ref, carry_b - hist_b, CONV_B_WIDTH, shift, rows, rows)
    bo = zb[:, d_b:2 * d_b] * u
    mixab_ref[0, :, d_a:d_a + d_b] = _rms(bo, gob_ref[...]).astype(BF16)
    nb_ref[0] = ubuf[rows:rows + carry_b, :]

    o_c = o_b + 3 * d_b
    q_ref[0] = _dot(h, win_ref[:, o_c:o_c + d_c])
    k = _dot(h, win_ref[:, o_c + d_c:o_c + 2 * d_c])
    v = _dot(h, win_ref[:, o_c + 2 * d_c:o_c + 3 * d_c])
    k_ref[0] = k
    v_ref[0] = v
    if first_kept_tile is not None:
        kept_k_ref, kept_v_ref = rest[:2]

        @pl.when(t >= first_kept_tile)
        def _():
            kept_k_ref[0] = k
            kept_v_ref[0] = v

    if n_tiles > 1:
        abuf[0:carry_a, :] = abuf[rows:rows + carry_a, :]
        ubuf[0:carry_b, :] = ubuf[rows:rows + carry_b, :]


def _mixin_call(x, mod, st_a, st_b, lw, *, rows, shift, keep):
    groups, n, d = x.shape
    rm = mod.shape[2]
    d_a = lw['conv_a_w'].shape[1]
    d_b = lw['conv_b_w'].shape[1]
    d_c = lw['g_out_c'].shape[1]
    p_in = lw['w_in'].shape[1]
    carry_a, carry_b = st_a.shape[1], st_b.shape[1]
    n_tiles = n // rows
    assert keep % rows == 0 and keep <= n
    first_kept_tile = (n - keep) // rows if keep < n else None
    kern = functools.partial(_mixin_kernel, rows=rows, shift=shift, d_a=d_a, d_b=d_b, d_c=d_c,
                             carry_a=carry_a, carry_b=carry_b, n_tiles=n_tiles, first_kept_tile=first_kept_tile)
    kept_specs, kept_shapes = [], []
    if first_kept_tile is not None:
        kept_spec = pl.BlockSpec((1, rows, d_c), lambda g, t: (g, jnp.maximum(t - first_kept_tile, 0), 0))
        kept_specs = [kept_spec, kept_spec]
        kept_shapes = [jax.ShapeDtypeStruct((groups, keep, d_c), F32)] * 2
    tile = lambda w: pl.BlockSpec((1, rows, w), lambda g, t: (g, t, 0))
    per_group = lambda r, w: pl.BlockSpec((1, r, w), lambda g, t: (g, 0, 0))
    outs = pl.pallas_call(
        kern,
        grid=(groups, n_tiles),
        in_specs=[
            tile(d),
            pl.BlockSpec((1, 6, rm, d), lambda g, t: (g, 0, 0, 0)),
            _const_spec((1, d)),
            _const_spec((d, p_in)),
            _const_spec((CONV_A_WIDTH, d_a)),
            _const_spec((1, d_a)),
            _const_spec((1, d_a)),
            _const_spec((1, d_a)),
            _const_spec((CONV_B_WIDTH, d_b)),
            _const_spec((1, d_a)),
            _const_spec((1, d_b)),
            per_group(carry_a, d_a),
            per_group(carry_b, d_b),
        ],
        out_specs=[
            tile(d_a + d_b), tile(d_c), tile(d_c), tile(d_c),
            per_group(carry_a, d_a), per_group(carry_b, d_b),
        ] + kept_specs,
        out_shape=[
            jax.ShapeDtypeStruct((groups, n, d_a + d_b), BF16),
            jax.ShapeDtypeStruct((groups, n, d_c), F32),
            jax.ShapeDtypeStruct((groups, n, d_c), F32),
            jax.ShapeDtypeStruct((groups, n, d_c), F32),
            jax.ShapeDtypeStruct((groups, carry_a, d_a), F32),
            jax.ShapeDtypeStruct((groups, carry_b, d_b), F32),
        ] + kept_shapes,
        scratch_shapes=[
            pltpu.VMEM((carry_a + rows, d_a), F32),
            pltpu.VMEM((carry_b + rows, d_b), F32),
        ],
        compiler_params=pltpu.CompilerParams(
            dimension_semantics=("arbitrary", "arbitrary"), vmem_limit_bytes=VMEM_LIMIT),
        name="mix_in",
    )(x, mod, lw['g_pre_mix'], lw['w_in'], lw['conv_a_w'], lw['conv_a_b'], lw['ln_a_g'], lw['ln_a_b'],
      lw['conv_b_w'], lw['g_out_a'], lw['g_out_b'], st_a, st_b)
    if first_kept_tile is None:
        outs = list(outs) + [outs[2], outs[3]]
    return outs


def _attn_kernel(q_ref, k_ref, v_ref, o_ref, qs0, qs1, kts, vs, acc, mrun, lrun, bias, pbuf, mbuf, *, seq):
    n_blk = seq // Q_BLK
    lane = lax.broadcasted_iota(jnp.int32, (1, LANES), 1)
    head0 = lane < HEAD_DIM
    scale = HEAD_DIM ** -0.5 * LOG2_E

    row = lax.broadcasted_iota(jnp.int32, (Q_BLK, 2 * Q_BLK), 0)
    col = lax.broadcasted_iota(jnp.int32, (Q_BLK, 2 * Q_BLK), 1)
    band = jnp.logical_and(col >= row, col <= row + Q_BLK)
    bias[0] = jnp.where(band, 0.0, NEG)
    bias[1] = jnp.where(jnp.logical_and(band, col >= Q_BLK), 0.0, NEG)
    vs[0:Q_BLK, 0:LANES] = jnp.zeros((Q_BLK, LANES), BF16)
    vs[:, LANES:2 * LANES] = jnp.ones((seq + Q_BLK, LANES), BF16)
    kts[0] = jnp.zeros((LANES, Q_BLK), BF16)

    order = DILATED_PATTERNS[::-1]
    n_grp = n_blk // ATTN_UNROLL
    for step, (_, dil) in enumerate(order):
        blk_per_phase = n_blk // dil

        def rows_of(j, dil=dil, blk_per_phase=blk_per_phase):
            if dil == 1:
                return pl.ds(pl.multiple_of(j * Q_BLK, Q_BLK), Q_BLK)
            phase = j // blk_per_phase
            i = j - phase * blk_per_phase
            return pl.ds(phase + i * (Q_BLK * dil), Q_BLK, stride=dil)

        def split(g, rows_of=rows_of):
            g = jnp.minimum(g, n_grp - 1)
            for u in range(ATTN_UNROLL):
                j = g * ATTN_UNROLL + u
                src = rows_of(j)
                dst = pl.ds(pl.multiple_of(j * Q_BLK, Q_BLK), Q_BLK)
                dst_kv = pl.ds(pl.multiple_of((j + 1) * Q_BLK, Q_BLK), Q_BLK)
                qv = q_ref.at[0][src, :] * scale
                qs0[dst, :] = jnp.where(head0, qv, 0.0).astype(BF16)
                qs1[dst, :] = jnp.where(head0, 0.0, qv).astype(BF16)
                kts[j + 1] = k_ref.at[0][src, :].T.astype(BF16)
                vs[dst_kv, 0:LANES] = v_ref.at[0][src, :].astype(BF16)

        def scores(g, slot, blk_per_phase=blk_per_phase):
            g = jnp.minimum(g, n_grp - 1)
            for u in range(ATTN_UNROLL):
                j = g * ATTN_UNROLL + u
                qrows = pl.ds(pl.multiple_of(j * Q_BLK, Q_BLK), Q_BLK)
                kbt = jnp.concatenate([kts[j], kts[j + 1]], axis=1)
                bb = bias[(j % blk_per_phase == 0).astype(jnp.int32)]
                ms = []
                for h, qs in enumerate((qs0, qs1)):
                    s = _dot(qs[qrows, :], kbt) + bb
                    m = jnp.max(s, axis=-1, keepdims=True)
                    pbuf[slot, u, h] = jnp.exp2(s - m).astype(BF16)
                    ms.append(m)
                mbuf[slot, u] = jnp.where(head0, ms[0], ms[1])

        def values(g, slot, step=step, rows_of=rows_of):
            for u in range(ATTN_UNROLL):
                j = g * ATTN_UNROLL + u
                vb = vs[pl.ds(pl.multiple_of(j * Q_BLK, Q_BLK), 2 * Q_BLK), :]
                pv0 = _dot(pbuf[slot, u, 0], vb)
                pv1 = _dot(pbuf[slot, u, 1], vb)
                a_new = jnp.where(head0, pv0[:, :LANES], pv1[:, :LANES])
                l_new = jnp.where(head0, pv0[:, LANES:], pv1[:, LANES:])
                m_new = mbuf[slot, u]
                dst = rows_of(j)
                if step > 0:
                    m_old = mrun[dst, :]
                    m_tot = jnp.maximum(m_old, m_new)
                    w_old = jnp.exp2(m_old - m_tot)
                    w_new = jnp.exp2(m_new - m_tot)
                    a_new = w_old * acc[dst, :] + w_new * a_new
                    l_new = w_old * lrun[dst, :] + w_new * l_new
                    m_new = m_tot
                if step < len(order) - 1:
                    mrun[dst, :] = m_new
                    lrun[dst, :] = l_new
                    acc[dst, :] = a_new
                else:
                    o_ref.at[0][dst, :] = a_new / l_new

        for g in range(3):
            split(jnp.int32(g))
        scores(jnp.int32(0), 0)

        def pair(t, carry, split=split, scores=scores, values=values):
            g = 2 * t
            scores(g + 1, 1)
            values(g, 0)
            scores(g + 2, 0)
            values(g + 1, 1)
            split(g + 3)
            split(g + 4)
            return carry

        lax.fori_loop(0, n_grp // 2, pair, 0)


def _attn_call(q, k, v):
    bsz, seq, d_c = q.shape
    assert seq % (Q_BLK * DILATED_PATTERNS[-1][1]) == 0 and seq % (2 * ATTN_UNROLL * Q_BLK) == 0
    spec = pl.BlockSpec((1, seq, LANES), lambda b, h: (b, 0, h))
    return pl.pallas_call(
        functools.partial(_attn_kernel, seq=seq),
        grid=(bsz, d_c // LANES),
        in_specs=[spec, spec, spec],
        out_specs=spec,
        out_shape=jax.ShapeDtypeStruct((bsz, seq, d_c), F32),
        scratch_shapes=[
            pltpu.VMEM((seq, LANES), BF16),
            pltpu.VMEM((seq, LANES), BF16),
            pltpu.VMEM((seq // Q_BLK + 1, LANES, Q_BLK), BF16),
            pltpu.VMEM((seq + Q_BLK, 2 * LANES), BF16),
            pltpu.VMEM((seq, LANES), F32),
            pltpu.VMEM((seq, LANES), F32),
            pltpu.VMEM((seq, LANES), F32),
            pltpu.VMEM((2, Q_BLK, 2 * Q_BLK), F32),
            pltpu.VMEM((2, ATTN_UNROLL, 2, Q_BLK, 2 * Q_BLK), BF16),
            pltpu.VMEM((2, ATTN_UNROLL, Q_BLK, LANES), F32),
        ],
        compiler_params=pltpu.CompilerParams(
            dimension_semantics=("arbitrary", "arbitrary"), vmem_limit_bytes=VMEM_LIMIT),
        name="attn_prompt",
    )(q, k, v)


def _multiplicity(dist):
    cnt = jnp.zeros(dist.shape, F32)
    for window, dil in DILATED_PATTERNS:
        hit = jnp.logical_and(dist >= 0, jnp.logical_and(dist <= window, dist % dil == 0))
        cnt = cnt + jnp.where(hit, 1.0, 0.0)
    return cnt


def _attn_sample_kernel(q_ref, kn_ref, vn_ref, kfar_ref, krec_ref, vfar_ref, vrec_ref, o_ref, cnt_c, cnt_n, *,
                        n_q, n_heads, w_buf, far_dil, near):
    n_rows = n_heads * n_q
    far_rows = kfar_ref.shape[2] * kfar_ref.shape[3]
    rec_rows = krec_ref.shape[2]
    n_c = far_rows + rec_rows

    @pl.when(pl.program_id(0) == 0)
    def _():
        def tables(n_cols, pos_and_head):
            r = lax.broadcasted_iota(jnp.int32, (n_rows, n_cols), 0)
            c = lax.broadcasted_iota(jnp.int32, (n_rows, n_cols), 1)
            pos, head = pos_and_head(c)
            q_pos = w_buf + r % n_q
            return jnp.where(head == r // n_q, _multiplicity(q_pos - pos), 0.0)

        def cached(c):
            grp = c // (n_q * n_heads)
            res = (c // n_heads) % n_q
            rec_pos = w_buf - near + (c - far_rows) // n_heads
            return jnp.where(c < far_rows, grp * far_dil + res, rec_pos), c % n_heads

        cnt_c[...] = tables(n_c, cached)
        cnt_n[...] = tables(n_rows, lambda c: (w_buf + c // n_heads, c % n_heads))

    q = (q_ref[0] * HEAD_DIM ** -0.5).astype(BF16)
    flat = lambda ref: ref[0, 0].reshape(far_rows, HEAD_DIM)
    k_c = jnp.concatenate([flat(kfar_ref), krec_ref[0, 0]], axis=0).astype(BF16)
    v_c = jnp.concatenate([flat(vfar_ref), vrec_ref[0, 0]], axis=0).astype(BF16)
    mult_c = cnt_c[...]
    mult_n = cnt_n[...]
    s_c = jnp.where(mult_c > 0.0, _dot_nt(q, k_c), NEG)
    s_n = jnp.where(mult_n > 0.0, _dot_nt(q, kn_ref[0].astype(BF16)), NEG)
    m = jnp.maximum(jnp.max(s_c, axis=-1, keepdims=True), jnp.max(s_n, axis=-1, keepdims=True))
    p_c = mult_c * jnp.exp(s_c - m)
    p_n = mult_n * jnp.exp(s_n - m)
    l = jnp.sum(p_c, axis=-1, keepdims=True) + jnp.sum(p_n, axis=-1, keepdims=True)
    o = _dot(p_c.astype(BF16), v_c) + _dot(p_n.astype(BF16), vn_ref[0].astype(BF16))
    o_ref[0] = o / l


def _attn_sample_call(q, k_new, v_new, cache_k, cache_v, layer):
    depth, bsz, w_buf, n_heads, head_dim = cache_k.shape
    n_rows = q.shape[1]
    n_q = n_rows // n_heads
    far_dil = DILATED_PATTERNS[-1][1]
    near = max(w for w, _ in DILATED_PATTERNS[:-1])
    assert w_buf % near == 0 and near % far_dil == 0 and near < w_buf and n_q <= far_dil
    n_far = (w_buf - near) // far_dil
    far_view = lambda c: c.reshape(depth, bsz, w_buf // far_dil, far_dil * n_heads, head_dim)
    rec_view = lambda c: c.reshape(depth, bsz, w_buf * n_heads, head_dim)
    new_spec = pl.BlockSpec((1, n_rows, head_dim), lambda b: (b, 0, 0))
    far_spec = pl.BlockSpec((1, 1, n_far, n_q * n_heads, head_dim), lambda b: (layer, b, 0, 0, 0))
    rec_spec = pl.BlockSpec((1, 1, near * n_heads, head_dim), lambda b: (layer, b, w_buf // near - 1, 0))
    n_c = n_far * n_q * n_heads + near * n_heads
    return pl.pallas_call(
        functools.partial(_attn_sample_kernel, n_q=n_q, n_heads=n_heads, w_buf=w_buf, far_dil=far_dil, near=near),
        grid=(bsz,),
        in_specs=[new_spec, new_spec, new_spec, far_spec, rec_spec, far_spec, rec_spec],
        out_specs=new_spec,
        out_shape=jax.ShapeDtypeStruct((bsz, n_rows, head_dim), F32),
        scratch_shapes=[pltpu.VMEM((n_rows, n_c), F32), pltpu.VMEM((n_rows, n_rows), F32)],
        compiler_params=pltpu.CompilerParams(
            dimension_semantics=("arbitrary",), vmem_limit_bytes=VMEM_LIMIT),
        name="attn_sample",
    )(q, k_new, v_new, far_view(cache_k), rec_view(cache_k), far_view(cache_v), rec_view(cache_v))


def _outffn_kernel(mixab_ref, o_ref, x_ref, mod_ref, goc_ref, wo_ref, gpm_ref, gpf_ref, wg_ref, wu_ref,
                   cfw_ref, wd_ref, gpo_ref, stf_ref,
                   y_ref, nf_ref, gbuf, *, rows, shift, carry_f, n_tiles):
    t = pl.program_id(1)

    @pl.when(t == 0)
    def _():
        gbuf[0:carry_f, :] = stf_ref[0]

    d_ab = mixab_ref.shape[2]
    oc = _rms(o_ref[0], goc_ref[...]).astype(BF16)
    y = _dot(mixab_ref[0], wo_ref[0:d_ab, :]) + _dot(oc, wo_ref[d_ab:, :])
    x1 = x_ref[0] + mod_ref[0, 2] * _rms(y, gpm_ref[...])

    h = (_rms(x1, gpf_ref[...]) * (1.0 + mod_ref[0, 4]) + mod_ref[0, 3]).astype(BF16)
    gbuf[carry_f:carry_f + rows, :] = _dot(h, wg_ref[...])
    hist = (FFN_CONV_WIDTH - 1) * shift
    g = _conv_rows(gbuf, cfw_ref, carry_f - hist, FFN_CONV_WIDTH, shift, rows, rows)
    f = (g * _sigmoid(g) * _dot(h, wu_ref[...])).astype(BF16)
    y2 = _dot(f, wd_ref[...])
    y_ref[0] = x1 + mod_ref[0, 5] * _rms(y2, gpo_ref[...])
    nf_ref[0] = gbuf[rows:rows + carry_f, :]
    if n_tiles > 1:
        gbuf[0:carry_f, :] = gbuf[rows:rows + carry_f, :]


def _outffn_call(mixab, o, x, mod, st_f, lw, *, rows, shift):
    groups, n, d = x.shape
    rm = mod.shape[2]
    d_ab = mixab.shape[2]
    d_c = o.shape[2]
    d_ff = lw['w_gate'].shape[1]
    carry_f = st_f.shape[1]
    n_tiles = n // rows
    kern = functools.partial(_outffn_kernel, rows=rows, shift=shift, carry_f=carry_f, n_tiles=n_tiles)
    tile = lambda w: pl.BlockSpec((1, rows, w), lambda g, t: (g, t, 0))
    per_group = lambda r, w: pl.BlockSpec((1, r, w), lambda g, t: (g, 0, 0))
    return pl.pallas_call(
        kern,
        grid=(groups, n_tiles),
        in_specs=[
            tile(d_ab), tile(d_c), tile(d),
            pl.BlockSpec((1, 6, rm, d), lambda g, t: (g, 0, 0, 0)),
            _const_spec((1, d_c)),
            _const_spec((d_ab + d_c, d)),
            _const_spec((1, d)),
            _const_spec((1, d)),
            _const_spec((d, d_ff)),
            _const_spec((d, d_ff)),
            _const_spec((FFN_CONV_WIDTH, d_ff)),
            _const_spec((d_ff, d)),
            _const_spec((1, d)),
            per_group(carry_f, d_ff),
        ],
        out_specs=[tile(d), per_group(carry_f, d_ff)],
        out_shape=[
            jax.ShapeDtypeStruct((groups, n, d), F32),
            jax.ShapeDtypeStruct((groups, carry_f, d_ff), F32),
        ],
        scratch_shapes=[pltpu.VMEM((carry_f + rows, d_ff), F32)],
        compiler_params=pltpu.CompilerParams(
            dimension_semantics=("arbitrary", "arbitrary"), vmem_limit_bytes=VMEM_LIMIT),
        name="out_ffn",
    )(mixab, o, x, mod, lw['g_out_c'], lw['w_o'], lw['g_post_mix'], lw['g_pre_ffn'], lw['w_gate'],
      lw['w_up'], lw['conv_f_w'], lw['w_down'], lw['g_post_ffn'], st_f)


def _front_pad(state, rows):
    return jnp.pad(state, ((0, 0), (rows - state.shape[1], 0), (0, 0)))


def _prompt_layer(x, mod, lw):
    bsz, seq, d = x.shape
    d_a, d_b = lw['conv_a_w'].shape[1], lw['conv_b_w'].shape[1]
    d_ff = lw['w_gate'].shape[1]
    rows = PROMPT_TILE_ROWS if seq % PROMPT_TILE_ROWS == 0 else seq
    carry_a = _round_up(CONV_A_WIDTH - 1, SUBLANES)
    carry_b = _round_up(CONV_B_WIDTH - 1, SUBLANES)
    carry_f = _round_up(FFN_CONV_WIDTH - 1, SUBLANES)
    mod4 = mod.reshape(bsz, 6, 1, d)
    keep = min(DILATED_PATTERNS[-1][0], seq)
    mixab, q, k, v, na, nb, kept_k, kept_v = _mixin_call(
        x, mod4, jnp.zeros((bsz, carry_a, d_a), F32), jnp.zeros((bsz, carry_b, d_b), F32), lw,
        rows=rows, shift=1, keep=keep)
    o = _attn_call(q, k, v)
    y, nf = _outffn_call(mixab, o, x, mod4, jnp.zeros((bsz, carry_f, d_ff), F32), lw, rows=rows, shift=1)
    n_heads = k.shape[2] // HEAD_DIM
    new_k = kept_k.reshape(bsz, keep, n_heads, HEAD_DIM)
    new_v = kept_v.reshape(bsz, keep, n_heads, HEAD_DIM)
    return (y, new_k, new_v, na[:, carry_a - (CONV_A_WIDTH - 1):], nb[:, carry_b - (CONV_B_WIDTH - 1):],
            nf[:, carry_f - (FFN_CONV_WIDTH - 1):])


def _time_major(state):
    bsz, k, c = state.shape
    return state.transpose(1, 0, 2).reshape(1, k * bsz, c)


def _batch_major(rows, bsz):
    _, n, c = rows.shape
    return rows.reshape(n // bsz, bsz, c).transpose(1, 0, 2)


def _sample_layer(x_tm, mod_tm, st_a, st_b, st_f, cache_k, cache_v, layer, lw, bsz):
    n = x_tm.shape[1]
    carry_a = _round_up((CONV_A_WIDTH - 1) * bsz, SUBLANES)
    carry_b = _round_up((CONV_B_WIDTH - 1) * bsz, SUBLANES)
    carry_f = _round_up((FFN_CONV_WIDTH - 1) * bsz, SUBLANES)
    mixab, q, k, v, na, nb, _, _ = _mixin_call(
        x_tm, mod_tm, _front_pad(_time_major(st_a), carry_a), _front_pad(_time_major(st_b), carry_b), lw,
        rows=n, shift=bsz, keep=n)
    qb, kb, vb = (_batch_major(a, bsz) for a in (q, k, v))
    n_q, n_heads = qb.shape[1], qb.shape[2] // HEAD_DIM
    q_hq = qb.reshape(bsz, n_q, n_heads, HEAD_DIM).transpose(0, 2, 1, 3).reshape(bsz, n_heads * n_q, HEAD_DIM)
    per_head = lambda a: a.reshape(bsz, n_q * n_heads, HEAD_DIM)
    o = _attn_sample_call(q_hq, per_head(kb), per_head(vb), cache_k, cache_v, layer)
    o = o.reshape(bsz, n_heads, n_q, HEAD_DIM).transpose(2, 0, 1, 3)
    o_tm = o.reshape(1, n, n_heads * HEAD_DIM)
    y, nf = _outffn_call(mixab, o_tm, x_tm, mod_tm, _front_pad(_time_major(st_f), carry_f), lw,
                         rows=n, shift=bsz)
    new_k = kb.reshape(bsz, kb.shape[1], n_heads, HEAD_DIM)
    new_v = vb.reshape(bsz, vb.shape[1], n_heads, HEAD_DIM)
    new_a = _batch_major(na[:, carry_a - (CONV_A_WIDTH - 1) * bsz:], bsz)
    new_b = _batch_major(nb[:, carry_b - (CONV_B_WIDTH - 1) * bsz:], bsz)
    new_f = _batch_major(nf[:, carry_f - (FFN_CONV_WIDTH - 1) * bsz:], bsz)
    return y, new_k, new_v, new_a, new_b, new_f


def kernel(x_prompt, x_sample, cache_k, cache_v, state_conv_a, state_conv_b, state_ffn_conv, c_prompt, c_sample, w_ada, b_ada, g_pre_mix, w_in, conv_a_w, conv_a_b, ln_a_g, ln_a_b, conv_b_w, g_out_a, g_out_b, g_out_c, w_o, g_post_mix, g_pre_ffn, w_gate, w_up, conv_f_w, w_down, g_post_ffn):
    depth = w_ada.shape[0]
    bsz_p, _, d = x_prompt.shape
    bsz_s, t_s, _ = x_sample.shape

    mod = _ada_call(jnp.concatenate([c_prompt, c_sample], axis=0), w_ada, b_ada)

    xp = x_prompt
    xs = x_sample.transpose(1, 0, 2).reshape(1, t_s * bsz_s, d)
    outs = [[] for _ in range(10)]
    for l in range(depth):
        row = lambda a: a[l][None, :]
        lw = {
            'g_pre_mix': row(g_pre_mix), 'w_in': w_in[l].astype(BF16),
            'conv_a_w': conv_a_w[l], 'conv_a_b': row(conv_a_b), 'ln_a_g': row(ln_a_g), 'ln_a_b': row(ln_a_b),
            'conv_b_w': conv_b_w[l], 'g_out_a': row(g_out_a), 'g_out_b': row(g_out_b), 'g_out_c': row(g_out_c),
            'w_o': w_o[l].astype(BF16), 'g_post_mix': row(g_post_mix), 'g_pre_ffn': row(g_pre_ffn),
            'w_gate': w_gate[l].astype(BF16), 'w_up': w_up[l].astype(BF16), 'conv_f_w': conv_f_w[l],
            'w_down': w_down[l].astype(BF16), 'g_post_ffn': row(g_post_ffn),
        }
        mod_p = mod[l, :bsz_p].reshape(bsz_p, 6, d)
        mod_s = mod[l, bsz_p:].reshape(bsz_s, 6, d).transpose(1, 0, 2)
        mod_s = jnp.tile(mod_s[:, None], (1, t_s, 1, 1)).reshape(1, 6, t_s * bsz_s, d)

        xp, kp, vp, ap, bp, fp = _prompt_layer(xp, mod_p, lw)
        xs, ks_, vs_, as_, bs_, fs_ = _sample_layer(
            xs, mod_s, state_conv_a[l], state_conv_b[l], state_ffn_conv[l], cache_k, cache_v, l, lw, bsz_s)
        for lst, val in zip(outs, (kp, vp, ks_, vs_, ap, as_, bp, bs_, fp, fs_)):
            lst.append(val)
    ys = xs.reshape(t_s, bsz_s, d).transpose(1, 0, 2)
    return (xp, ys) + tuple(jnp.stack(o) for o in outs)
```

```python
import functools

import jax
import jax.numpy as jnp
from jax import lax
from jax.experimental import pallas as pl
from jax.experimental.pallas import tpu as pltpu

F32 = jnp.float32
BF16 = jnp.bfloat16

HEAD_DIM = 64
LANES = 128
SUBLANES = 8
CONV_A_WIDTH = 31
CONV_B_WIDTH = 3
FFN_CONV_WIDTH = 3
DILATED_PATTERNS = ((128, 1), (512, 4), (2048, 16))
Q_BLK = 128
EPS = 1e-6
NEG = -1e30
LOG2_E = 1.4426950408889634
ATTN_UNROLL = 4
VMEM_LIMIT = 56 * 1024 * 1024
PROMPT_TILE_ROWS = 512
CONV_CHUNK_ROWS = 64


def _round_up(n, m):
    return (n + m - 1) // m * m


def _rms(x, g):
    return x * lax.rsqrt(jnp.mean(x * x, axis=-1, keepdims=True) + EPS) * g


def _sigmoid(x):
    return 1.0 / (1.0 + jnp.exp(-x))


def _dot(a, b):
    return jnp.dot(a, b, preferred_element_type=F32)


def _dot_nt(a, b):
    return lax.dot_general(a, b, (((1,), (1,)), ((), ())), preferred_element_type=F32)


def _const_spec(shape):
    zeros = (0,) * len(shape)
    return pl.BlockSpec(shape, lambda *_: zeros, pipeline_mode=pl.Buffered(1))


def _ada_kernel(c_ref, w_ref, b_ref, o_ref):
    c = c_ref[...]
    s = (c * _sigmoid(c)).astype(BF16)
    o_ref[0] = _dot(s, w_ref[0].astype(BF16)) + b_ref[0]


def _ada_call(c_all, w_ada, b_ada):
    depth, d, n = w_ada.shape
    rows = c_all.shape[0]
    tn = n // 4
    return pl.pallas_call(
        _ada_kernel,
        grid=(depth, n // tn),
        in_specs=[
            pl.BlockSpec((rows, d), lambda l, j: (0, 0)),
            pl.BlockSpec((1, d, tn), lambda l, j: (l, 0, j)),
            pl.BlockSpec((1, 1, tn), lambda l, j: (l, 0, j)),
        ],
        out_specs=pl.BlockSpec((1, rows, tn), lambda l, j: (l, 0, j)),
        out_shape=jax.ShapeDtypeStruct((depth, rows, n), F32),
        compiler_params=pltpu.CompilerParams(
            dimension_semantics=("arbitrary", "arbitrary"), vmem_limit_bytes=VMEM_LIMIT),
        name="ada_mod",
    )(c_all, w_ada, b_ada.reshape(depth, 1, n))


def _tap_groups(base, ntaps, shift):
    by_residue = {}
    for j in range(ntaps):
        off = base + j * shift
        by_residue.setdefault(off % SUBLANES, []).append((j, off - off % SUBLANES))
    return sorted(by_residue.items())


def _conv_chunk(buf_ref, w_ref, tap_groups, c0, chunk):
    acc = None
    for res, taps in tap_groups:
        ext = chunk + (SUBLANES if res else 0)
        part = None
        for j, off in taps:
            term = buf_ref[pl.ds(c0 + off, ext), :] * w_ref[j:j + 1, :]
            part = term if part is None else part + term
        if res:
            part = part[res:res + chunk]
        acc = part if acc is None else acc + part
    return acc


def _mixin_kernel(x_ref, mod_ref, gpre_ref, win_ref, caw_ref, cab_ref, lng_ref, lnb_ref, cbw_ref,
                  goa_ref, gob_ref, sta_ref, stb_ref, *rest,
                  rows, shift, d_a, d_b, d_c, carry_a, carry_b, n_tiles, first_kept_tile, n_prev):
    prev_refs, rest = (rest[:2], rest[2:]) if n_prev else ((), rest)
    mixab_ref, q_ref, k_ref, v_ref, na_ref, nb_ref = rest[:6]
    abuf, ubuf = rest[-2:]
    t = pl.program_id(1)

    @pl.when(t == 0)
    def _():
        abuf[0:carry_a, :] = sta_ref[0]
        ubuf[0:carry_b, :] = stb_ref[0]

    x = x_ref[0]
    h = (_rms(x, gpre_ref[...]) * (1.0 + mod_ref[0, 1]) + mod_ref[0, 0]).astype(BF16)

    o_b = 2 * d_a
    o_c = o_b + 3 * d_b
    za = _dot(h, win_ref[:, 0:2 * d_a])
    abuf[carry_a:carry_a + rows, :] = za[:, 0:d_a] * _sigmoid(za[:, d_a:2 * d_a])
    zb = _dot(h, win_ref[:, o_b:o_b + 3 * d_b])
    ubuf[carry_b:carry_b + rows, :] = zb[:, 2 * d_b:3 * d_b] * zb[:, 0:d_b]
    b_gate = zb[:, d_b:2 * d_b]

    q_ref[0] = _dot(h, win_ref[:, o_c:o_c + d_c])
    k = _dot(h, win_ref[:, o_c + d_c:o_c + 2 * d_c])
    v = _dot(h, win_ref[:, o_c + 2 * d_c:o_c + 3 * d_c])
    k_ref[0] = k
    v_ref[0] = v

    taps_a = _tap_groups(carry_a - (CONV_A_WIDTH - 1) * shift, CONV_A_WIDTH, shift)
    chunk = min(rows, CONV_CHUNK_ROWS)
    a = jnp.concatenate([_conv_chunk(abuf, caw_ref, taps_a, c0, chunk) for c0 in range(0, rows, chunk)], axis=0)
    a = a + cab_ref[...]
    mu = jnp.mean(a, axis=-1, keepdims=True)
    ac = a - mu
    var = jnp.mean(ac * ac, axis=-1, keepdims=True)
    a = ac * lax.rsqrt(var + EPS) * lng_ref[...] + lnb_ref[...]
    a = a * _sigmoid(a)
    mixab_ref[0, :, 0:d_a] = _rms(a, goa_ref[...]).astype(BF16)
    na_ref[0] = abuf[rows:rows + carry_a, :]

    taps_b = _tap_groups(carry_b - (CONV_B_WIDTH - 1) * shift, CONV_B_WIDTH, shift)
    u = _conv_chunk(ubuf, cbw_ref, taps_b, 0, rows)
    mixab_ref[0, :, d_a:d_a + d_b] = _rms(b_gate * u, gob_ref[...]).astype(BF16)
    nb_ref[0] = ubuf[rows:rows + carry_b, :]

    if first_kept_tile is not None:
        kept_kt_ref, kept_vt_ref = rest[6:8]

        @pl.when(t >= first_kept_tile)
        def _():
            for idx, (kept_ref, new) in enumerate(((kept_kt_ref, k), (kept_vt_ref, v))):
                for i in range(n_prev):
                    kept_ref[i, 0] = prev_refs[idx][i, 0]
                kept_ref[n_prev, 0] = new.T

    if n_tiles > 1:
        abuf[0:carry_a, :] = abuf[rows:rows + carry_a, :]
        ubuf[0:carry_b, :] = ubuf[rows:rows + carry_b, :]


def _mixin_call(x, mod, st_a, st_b, lw, *, rows, shift, keep, prev_kept=()):
    groups, n, d = x.shape
    rm = mod.shape[2]
    d_a = lw['conv_a_w'].shape[1]
    d_b = lw['conv_b_w'].shape[1]
    d_c = lw['g_out_c'].shape[1]
    p_in = lw['w_in'].shape[1]
    carry_a, carry_b = st_a.shape[1], st_b.shape[1]
    n_tiles = n // rows
    assert keep % rows == 0 and keep <= n
    first_kept_tile = (n - keep) // rows if keep < n else None
    n_prev = prev_kept[0].shape[0] if prev_kept else 0
    kern = functools.partial(_mixin_kernel, rows=rows, shift=shift, d_a=d_a, d_b=d_b, d_c=d_c,
                             carry_a=carry_a, carry_b=carry_b, n_tiles=n_tiles, first_kept_tile=first_kept_tile,
                             n_prev=n_prev)
    kept_specs, kept_shapes, prev_specs = [], [], []
    if first_kept_tile is not None:
        kept_map = lambda g, t: (0, g, 0, jnp.maximum(t - first_kept_tile, 0))
        kept_specs = [pl.BlockSpec((n_prev + 1, 1, d_c, rows), kept_map)] * 2
        kept_shapes = [jax.ShapeDtypeStruct((n_prev + 1, groups, d_c, keep), F32)] * 2
        prev_specs = [pl.BlockSpec((n_prev, 1, d_c, rows), kept_map)] * (2 if n_prev else 0)
    tile = lambda w: pl.BlockSpec((1, rows, w), lambda g, t: (g, t, 0))
    per_group = lambda r, w: pl.BlockSpec((1, r, w), lambda g, t: (g, 0, 0))
    outs = pl.pallas_call(
        kern,
        grid=(groups, n_tiles),
        in_specs=[
            tile(d),
            pl.BlockSpec((1, 6, rm, d), lambda g, t: (g, 0, 0, 0)),
            _const_spec((1, d)),
            _const_spec((d, p_in)),
            _const_spec((CONV_A_WIDTH, d_a)),
            _const_spec((1, d_a)),
            _const_spec((1, d_a)),
            _const_spec((1, d_a)),
            _const_spec((CONV_B_WIDTH, d_b)),
            _const_spec((1, d_a)),
            _const_spec((1, d_b)),
            per_group(carry_a, d_a),
            per_group(carry_b, d_b),
        ] + prev_specs,
        out_specs=[
            tile(d_a + d_b), tile(d_c), tile(d_c), tile(d_c),
            per_group(carry_a, d_a), per_group(carry_b, d_b),
        ] + kept_specs,
        out_shape=[
            jax.ShapeDtypeStruct((groups, n, d_a + d_b), BF16),
            jax.ShapeDtypeStruct((groups, n, d_c), F32),
            jax.ShapeDtypeStruct((groups, n, d_c), F32),
            jax.ShapeDtypeStruct((groups, n, d_c), F32),
            jax.ShapeDtypeStruct((groups, carry_a, d_a), F32),
            jax.ShapeDtypeStruct((groups, carry_b, d_b), F32),
        ] + kept_shapes,
        scratch_shapes=[
            pltpu.VMEM((carry_a + rows, d_a), F32),
            pltpu.VMEM((carry_b + rows, d_b), F32),
        ],
        compiler_params=pltpu.CompilerParams(
            dimension_semantics=("arbitrary", "arbitrary"), vmem_limit_bytes=VMEM_LIMIT),
        name="mix_in",
    )(x, mod, lw['g_pre_mix'], lw['w_in'], lw['conv_a_w'], lw['conv_a_b'], lw['ln_a_g'], lw['ln_a_b'],
      lw['conv_b_w'], lw['g_out_a'], lw['g_out_b'], st_a, st_b, *prev_kept)
    return outs


def _attn_kernel(q_ref, k_ref, v_ref, o_ref, qs0, qs1, kts, vs, acc, mrun, lrun, bias, pbuf, mbuf, *, seq):
    n_blk = seq // Q_BLK
    lane = lax.broadcasted_iota(jnp.int32, (1, LANES), 1)
    head0 = lane < HEAD_DIM
    scale = HEAD_DIM ** -0.5 * LOG2_E

    row = lax.broadcasted_iota(jnp.int32, (Q_BLK, 2 * Q_BLK), 0)
    col = lax.broadcasted_iota(jnp.int32, (Q_BLK, 2 * Q_BLK), 1)
    band = jnp.logical_and(col >= row, col <= row + Q_BLK)
    bias[0] = jnp.where(band, 0.0, NEG)
    bias[1] = jnp.where(jnp.logical_and(band, col >= Q_BLK), 0.0, NEG)
    vs[0:Q_BLK, 0:LANES] = jnp.zeros((Q_BLK, LANES), BF16)
    vs[:, LANES:2 * LANES] = jnp.ones((seq + Q_BLK, LANES), BF16)
    kts[0] = jnp.zeros((LANES, Q_BLK), BF16)

    order = DILATED_PATTERNS[::-1]
    n_grp = n_blk // ATTN_UNROLL
    for step, (_, dil) in enumerate(order):
        blk_per_phase = n_blk // dil

        def rows_of(j, dil=dil, blk_per_phase=blk_per_phase):
            if dil == 1:
                return pl.ds(pl.multiple_of(j * Q_BLK, Q_BLK), Q_BLK)
            phase = j // blk_per_phase
            i = j - phase * blk_per_phase
            return pl.ds(phase + i * (Q_BLK * dil), Q_BLK, stride=dil)

        def split(g, rows_of=rows_of):
            g = jnp.minimum(g, n_grp - 1)
            for u in range(ATTN_UNROLL):
                j = g * ATTN_UNROLL + u
                src = rows_of(j)
                dst = pl.ds(pl.multiple_of(j * Q_BLK, Q_BLK), Q_BLK)
                dst_kv = pl.ds(pl.multiple_of((j + 1) * Q_BLK, Q_BLK), Q_BLK)
                qv = q_ref.at[0][src, :] * scale
                qs0[dst, :] = jnp.where(head0, qv, 0.0).astype(BF16)
                qs1[dst, :] = jnp.where(head0, 0.0, qv).astype(BF16)
                kts[j + 1] = k_ref.at[0][src, :].T.astype(BF16)
                vs[dst_kv, 0:LANES] = v_ref.at[0][src, :].astype(BF16)

        def scores(g, slot, blk_per_phase=blk_per_phase):
            g = jnp.minimum(g, n_grp - 1)
            for u in range(ATTN_UNROLL):
                j = g * ATTN_UNROLL + u
                qrows = pl.ds(pl.multiple_of(j * Q_BLK, Q_BLK), Q_BLK)
                kbt = jnp.concatenate([kts[j], kts[j + 1]], axis=1)
                bb = bias[(j % blk_per_phase == 0).astype(jnp.int32)]
                ms = []
                for h, qs in enumerate((qs0, qs1)):
                    s = _dot(qs[qrows, :], kbt) + bb
                    m = jnp.max(s, axis=-1, keepdims=True)
                    pbuf[slot, u, h] = jnp.exp2(s - m).astype(BF16)
                    ms.append(m)
                mbuf[slot, u] = jnp.where(head0, ms[0], ms[1])

        def values(g, slot, step=step, rows_of=rows_of):
            for u in range(ATTN_UNROLL):
                j = g * ATTN_UNROLL + u
                vb = vs[pl.ds(pl.multiple_of(j * Q_BLK, Q_BLK), 2 * Q_BLK), :]
                pv0 = _dot(pbuf[slot, u, 0], vb)
                pv1 = _dot(pbuf[slot, u, 1], vb)
                a_new = jnp.where(head0, pv0[:, :LANES], pv1[:, :LANES])
                l_new = jnp.where(head0, pv0[:, LANES:], pv1[:, LANES:])
                m_new = mbuf[slot, u]
                dst = rows_of(j)
                if step > 0:
                    m_old = mrun[dst, :]
                    m_tot = jnp.maximum(m_old, m_new)
                    w_old = jnp.exp2(m_old - m_tot)
                    w_new = jnp.exp2(m_new - m_tot)
                    a_new = w_old * acc[dst, :] + w_new * a_new
                    l_new = w_old * lrun[dst, :] + w_new * l_new
                    m_new = m_tot
                if step < len(order) - 1:
                    mrun[dst, :] = m_new
                    lrun[dst, :] = l_new
                    acc[dst, :] = a_new
                else:
                    o_ref.at[0][dst, :] = a_new / l_new

        for g in range(3):
            split(jnp.int32(g))
        scores(jnp.int32(0), 0)

        def pair(t, carry, split=split, scores=scores, values=values):
            g = 2 * t
            scores(g + 1, 1)
            values(g, 0)
            scores(g + 2, 0)
            values(g + 1, 1)
            split(g + 3)
            split(g + 4)
            return carry

        lax.fori_loop(0, n_grp // 2, pair, 0)


def _attn_call(q, k, v):
    bsz, seq, d_c = q.shape
    assert seq % (Q_BLK * DILATED_PATTERNS[-1][1]) == 0 and seq % (2 * ATTN_UNROLL * Q_BLK) == 0
    spec = pl.BlockSpec((1, seq, LANES), lambda b, h: (b, 0, h))
    return pl.pallas_call(
        functools.partial(_attn_kernel, seq=seq),
        grid=(bsz, d_c // LANES),
        in_specs=[spec, spec, spec],
        out_specs=spec,
        out_shape=jax.ShapeDtypeStruct((bsz, seq, d_c), F32),
        scratch_shapes=[
            pltpu.VMEM((seq, LANES), BF16),
            pltpu.VMEM((seq, LANES), BF16),
            pltpu.VMEM((seq // Q_BLK + 1, LANES, Q_BLK), BF16),
            pltpu.VMEM((seq + Q_BLK, 2 * LANES), BF16),
            pltpu.VMEM((seq, LANES), F32),
            pltpu.VMEM((seq, LANES), F32),
            pltpu.VMEM((seq, LANES), F32),
            pltpu.VMEM((2, Q_BLK, 2 * Q_BLK), F32),
            pltpu.VMEM((2, ATTN_UNROLL, 2, Q_BLK, 2 * Q_BLK), BF16),
            pltpu.VMEM((2, ATTN_UNROLL, Q_BLK, LANES), F32),
        ],
        compiler_params=pltpu.CompilerParams(
            dimension_semantics=("arbitrary", "arbitrary"), vmem_limit_bytes=VMEM_LIMIT),
        name="attn_prompt",
    )(q, k, v)


def _multiplicity(dist):
    cnt = jnp.zeros(dist.shape, F32)
    for window, dil in DILATED_PATTERNS:
        hit = jnp.logical_and(dist >= 0, jnp.logical_and(dist <= window, dist % dil == 0))
        cnt = cnt + jnp.where(hit, 1.0, 0.0)
    return cnt


def _attn_sample_kernel(q_ref, kn_ref, vn_ref, kt_ref, vt_ref, o_ref, cnt_c, cnt_n, *, n_q, n_heads, w_buf):
    @pl.when(pl.program_id(0) == 0)
    def _():
        def table(n_keys, first_pos):
            q_pos = w_buf + lax.broadcasted_iota(jnp.int32, (n_q, n_keys), 0)
            k_pos = first_pos + lax.broadcasted_iota(jnp.int32, (n_q, n_keys), 1)
            return _multiplicity(q_pos - k_pos)

        cnt_c[...] = table(w_buf, 0)
        cnt_n[...] = table(n_q, w_buf)

    mult_c = cnt_c[...]
    mult_n = cnt_n[...]
    for h in range(n_heads):
        q = (q_ref[0, h] * HEAD_DIM ** -0.5).astype(BF16)
        s_c = jnp.where(mult_c > 0.0, _dot(q, kt_ref[0, 0, h].astype(BF16)), NEG)
        s_n = jnp.where(mult_n > 0.0, _dot_nt(q, kn_ref[0, h].astype(BF16)), NEG)
        m = jnp.maximum(jnp.max(s_c, axis=-1, keepdims=True), jnp.max(s_n, axis=-1, keepdims=True))
        p_c = mult_c * jnp.exp(s_c - m)
        p_n = mult_n * jnp.exp(s_n - m)
        l = jnp.sum(p_c, axis=-1, keepdims=True) + jnp.sum(p_n, axis=-1, keepdims=True)
        o = _dot_nt(p_c.astype(BF16), vt_ref[0, 0, h].astype(BF16)) + _dot(p_n.astype(BF16), vn_ref[0, h].astype(BF16))
        o_ref[0, h] = o / l


def _attn_sample_call(q, k_new, v_new, cache_kt, cache_vt, layer):
    bsz, n_heads, n_q, head_dim = q.shape
    w_buf = cache_kt.shape[4]
    new_spec = pl.BlockSpec((1, n_heads, n_q, head_dim), lambda b: (b, 0, 0, 0))
    cache_spec = pl.BlockSpec((1, 1, n_heads, head_dim, w_buf), lambda b: (layer, b, 0, 0, 0))
    return pl.pallas_call(
        functools.partial(_attn_sample_kernel, n_q=n_q, n_heads=n_heads, w_buf=w_buf),
        grid=(bsz,),
        in_specs=[new_spec, new_spec, new_spec, cache_spec, cache_spec],
        out_specs=new_spec,
        out_shape=jax.ShapeDtypeStruct(q.shape, F32),
        scratch_shapes=[pltpu.VMEM((n_q, w_buf), F32), pltpu.VMEM((n_q, n_q), F32)],
        compiler_params=pltpu.CompilerParams(
            dimension_semantics=("arbitrary",), vmem_limit_bytes=VMEM_LIMIT),
        name="attn_sample",
    )(q, k_new, v_new, cache_kt, cache_vt)


def _outffn_kernel(mixab_ref, o_ref, x_ref, mod_ref, goc_ref, wo_ref, gpm_ref, gpf_ref, wg_ref, wu_ref,
                   cfw_ref, wd_ref, gpo_ref, stf_ref,
                   y_ref, nf_ref, gbuf, *, rows, shift, carry_f, n_tiles):
    t = pl.program_id(1)

    @pl.when(t == 0)
    def _():
        gbuf[0:carry_f, :] = stf_ref[0]

    d_ab = mixab_ref.shape[2]
    oc = _rms(o_ref[0], goc_ref[...]).astype(BF16)
    y = _dot(mixab_ref[0], wo_ref[0:d_ab, :]) + _dot(oc, wo_ref[d_ab:, :])
    x1 = x_ref[0] + mod_ref[0, 2] * _rms(y, gpm_ref[...])

    h = (_rms(x1, gpf_ref[...]) * (1.0 + mod_ref[0, 4]) + mod_ref[0, 3]).astype(BF16)
    gbuf[carry_f:carry_f + rows, :] = _dot(h, wg_ref[...])
    taps = _tap_groups(carry_f - (FFN_CONV_WIDTH - 1) * shift, FFN_CONV_WIDTH, shift)
    g = _conv_chunk(gbuf, cfw_ref, taps, 0, rows)
    f = (g * _sigmoid(g) * _dot(h, wu_ref[...])).astype(BF16)
    y2 = _dot(f, wd_ref[...])
    y_ref[0] = x1 + mod_ref[0, 5] * _rms(y2, gpo_ref[...])
    nf_ref[0] = gbuf[rows:rows + carry_f, :]
    if n_tiles > 1:
        gbuf[0:carry_f, :] = gbuf[rows:rows + carry_f, :]


def _outffn_call(mixab, o, x, mod, st_f, lw, *, rows, shift):
    groups, n, d = x.shape
    rm = mod.shape[2]
    d_ab = mixab.shape[2]
    d_c = o.shape[2]
    d_ff = lw['w_gate'].shape[1]
    carry_f = st_f.shape[1]
    n_tiles = n // rows
    kern = functools.partial(_outffn_kernel, rows=rows, shift=shift, carry_f=carry_f, n_tiles=n_tiles)
    tile = lambda w: pl.BlockSpec((1, rows, w), lambda g, t: (g, t, 0))
    per_group = lambda r, w: pl.BlockSpec((1, r, w), lambda g, t: (g, 0, 0))
    return pl.pallas_call(
        kern,
        grid=(groups, n_tiles),
        in_specs=[
            tile(d_ab), tile(d_c), tile(d),
            pl.BlockSpec((1, 6, rm, d), lambda g, t: (g, 0, 0, 0)),
            _const_spec((1, d_c)),
            _const_spec((d_ab + d_c, d)),
            _const_spec((1, d)),
            _const_spec((1, d)),
            _const_spec((d, d_ff)),
            _const_spec((d, d_ff)),
            _const_spec((FFN_CONV_WIDTH, d_ff)),
            _const_spec((d_ff, d)),
            _const_spec((1, d)),
            per_group(carry_f, d_ff),
        ],
        out_specs=[tile(d), per_group(carry_f, d_ff)],
        out_shape=[
            jax.ShapeDtypeStruct((groups, n, d), F32),
            jax.ShapeDtypeStruct((groups, carry_f, d_ff), F32),
        ],
        scratch_shapes=[pltpu.VMEM((carry_f + rows, d_ff), F32)],
        compiler_params=pltpu.CompilerParams(
            dimension_semantics=("arbitrary", "arbitrary"), vmem_limit_bytes=VMEM_LIMIT),
        name="out_ffn",
    )(mixab, o, x, mod, lw['g_out_c'], lw['w_o'], lw['g_post_mix'], lw['g_pre_ffn'], lw['w_gate'],
      lw['w_up'], lw['conv_f_w'], lw['w_down'], lw['g_post_ffn'], st_f)


def _front_pad(state, rows):
    return jnp.pad(state, ((0, 0), (rows - state.shape[1], 0), (0, 0)))


def _prompt_layer(x, mod, lw, prev_kept):
    bsz, seq, d = x.shape
    d_a, d_b = lw['conv_a_w'].shape[1], lw['conv_b_w'].shape[1]
    d_ff = lw['w_gate'].shape[1]
    rows = PROMPT_TILE_ROWS if seq % PROMPT_TILE_ROWS == 0 else seq
    carry_a = _round_up(CONV_A_WIDTH - 1, SUBLANES)
    carry_b = _round_up(CONV_B_WIDTH - 1, SUBLANES)
    carry_f = _round_up(FFN_CONV_WIDTH - 1, SUBLANES)
    mod4 = mod.reshape(bsz, 6, 1, d)
    keep = min(DILATED_PATTERNS[-1][0], seq)
    assert keep < seq
    mixab, q, k, v, na, nb, kept_kt, kept_vt = _mixin_call(
        x, mod4, jnp.zeros((bsz, carry_a, d_a), F32), jnp.zeros((bsz, carry_b, d_b), F32), lw,
        rows=rows, shift=1, keep=keep, prev_kept=prev_kept)
    o = _attn_call(q, k, v)
    y, nf = _outffn_call(mixab, o, x, mod4, jnp.zeros((bsz, carry_f, d_ff), F32), lw, rows=rows, shift=1)
    return (y, (kept_kt, kept_vt), na[:, carry_a - (CONV_A_WIDTH - 1):], nb[:, carry_b - (CONV_B_WIDTH - 1):],
            nf[:, carry_f - (FFN_CONV_WIDTH - 1):])


def _time_major(state):
    bsz, k, c = state.shape
    return state.transpose(1, 0, 2).reshape(1, k * bsz, c)


def _batch_major(rows, bsz):
    _, n, c = rows.shape
    return rows.reshape(n // bsz, bsz, c).transpose(1, 0, 2)


def _sample_layer(x_tm, mod_tm, st_a, st_b, st_f, cache_kt, cache_vt, layer, lw, bsz):
    n = x_tm.shape[1]
    carry_a = _round_up((CONV_A_WIDTH - 1) * bsz, SUBLANES)
    carry_b = _round_up((CONV_B_WIDTH - 1) * bsz, SUBLANES)
    carry_f = _round_up((FFN_CONV_WIDTH - 1) * bsz, SUBLANES)
    mixab, q, k, v, na, nb = _mixin_call(
        x_tm, mod_tm, _front_pad(_time_major(st_a), carry_a), _front_pad(_time_major(st_b), carry_b), lw,
        rows=n, shift=bsz, keep=n)
    qb, kb, vb = (_batch_major(a, bsz) for a in (q, k, v))
    n_q, n_heads = qb.shape[1], qb.shape[2] // HEAD_DIM
    per_head = lambda a: a.reshape(bsz, n_q, n_heads, HEAD_DIM).transpose(0, 2, 1, 3)
    o = _attn_sample_call(per_head(qb), per_head(kb), per_head(vb), cache_kt, cache_vt, layer)
    o_tm = o.transpose(2, 0, 1, 3).reshape(1, n, n_heads * HEAD_DIM)
    y, nf = _outffn_call(mixab, o_tm, x_tm, mod_tm, _front_pad(_time_major(st_f), carry_f), lw,
                         rows=n, shift=bsz)
    new_k = kb.reshape(bsz, kb.shape[1], n_heads, HEAD_DIM)
    new_v = vb.reshape(bsz, vb.shape[1], n_heads, HEAD_DIM)
    new_a = _batch_major(na[:, carry_a - (CONV_A_WIDTH - 1) * bsz:], bsz)
    new_b = _batch_major(nb[:, carry_b - (CONV_B_WIDTH - 1) * bsz:], bsz)
    new_f = _batch_major(nf[:, carry_f - (FFN_CONV_WIDTH - 1) * bsz:], bsz)
    return y, new_k, new_v, new_a, new_b, new_f


def kernel(x_prompt, x_sample, cache_k, cache_v, state_conv_a, state_conv_b, state_ffn_conv, c_prompt, c_sample, w_ada, b_ada, g_pre_mix, w_in, conv_a_w, conv_a_b, ln_a_g, ln_a_b, conv_b_w, g_out_a, g_out_b, g_out_c, w_o, g_post_mix, g_pre_ffn, w_gate, w_up, conv_f_w, w_down, g_post_ffn):
    depth = w_ada.shape[0]
    bsz_p, _, d = x_prompt.shape
    bsz_s, t_s, _ = x_sample.shape

    mod = _ada_call(jnp.concatenate([c_prompt, c_sample], axis=0), w_ada, b_ada)

    cache_kt = cache_k.transpose(0, 1, 3, 4, 2)
    cache_vt = cache_v.transpose(0, 1, 3, 4, 2)

    xp = x_prompt
    xs = x_sample.transpose(1, 0, 2).reshape(1, t_s * bsz_s, d)
    outs = [[] for _ in range(8)]
    kept = ()
    for l in range(depth):
        row = lambda a: a[l][None, :]
        lw = {
            'g_pre_mix': row(g_pre_mix), 'w_in': w_in[l].astype(BF16),
            'conv_a_w': conv_a_w[l], 'conv_a_b': row(conv_a_b), 'ln_a_g': row(ln_a_g), 'ln_a_b': row(ln_a_b),
            'conv_b_w': conv_b_w[l], 'g_out_a': row(g_out_a), 'g_out_b': row(g_out_b), 'g_out_c': row(g_out_c),
            'w_o': w_o[l].astype(BF16), 'g_post_mix': row(g_post_mix), 'g_pre_ffn': row(g_pre_ffn),
            'w_gate': w_gate[l].astype(BF16), 'w_up': w_up[l].astype(BF16), 'conv_f_w': conv_f_w[l],
            'w_down': w_down[l].astype(BF16), 'g_post_ffn': row(g_post_ffn),
        }
        mod_p = mod[l, :bsz_p].reshape(bsz_p, 6, d)
        mod_s = mod[l, bsz_p:].reshape(bsz_s, 6, d).transpose(1, 0, 2)
        mod_s = jnp.tile(mod_s[:, None], (1, t_s, 1, 1)).reshape(1, 6, t_s * bsz_s, d)

        xp, kept, ap, bp, fp = _prompt_layer(xp, mod_p, lw, kept)
        xs, ks_, vs_, as_, bs_, fs_ = _sample_layer(
            xs, mod_s, state_conv_a[l], state_conv_b[l], state_ffn_conv[l], cache_kt, cache_vt, l, lw, bsz_s)
        for lst, val in zip(outs, (ks_, vs_, ap, as_, bp, bs_, fp, fs_)):
            lst.append(val)
    ys = xs.reshape(t_s, bsz_s, d).transpose(1, 0, 2)
    kp, vp = (a.reshape(a.shape[:2] + (-1, HEAD_DIM, a.shape[3])).transpose(0, 1, 4, 2, 3) for a in kept)
    return (xp, ys, kp, vp) + tuple(jnp.stack(o) for o in outs)
```

```python
import functools

import jax
import jax.numpy as jnp
from jax import lax
from jax.experimental import pallas as pl
from jax.experimental.pallas import tpu as pltpu

F32 = jnp.float32
BF16 = jnp.bfloat16

HEAD_DIM = 64
LANES = 128
SUBLANES = 8
CONV_A_WIDTH = 31
CONV_B_WIDTH = 3
FFN_CONV_WIDTH = 3
DILATED_PATTERNS = ((128, 1), (512, 4), (2048, 16))
Q_BLK = 128
EPS = 1e-6
NEG = -1e30
LOG2_E = 1.4426950408889634
ATTN_UNROLL = 4
VMEM_LIMIT = 56 * 1024 * 1024
PROMPT_TILE_ROWS = 512
CONV_CHUNK_ROWS = 64


def _round_up(n, m):
    return (n + m - 1) // m * m


def _rms(x, g):
    return x * lax.rsqrt(jnp.mean(x * x, axis=-1, keepdims=True) + EPS) * g


def _sigmoid(x):
    return 1.0 / (1.0 + jnp.exp(-x))


def _dot(a, b):
    return jnp.dot(a, b, preferred_element_type=F32)


def _dot_nt(a, b):
    return lax.dot_general(a, b, (((1,), (1,)), ((), ())), preferred_element_type=F32)


def _const_spec(shape):
    zeros = (0,) * len(shape)
    return pl.BlockSpec(shape, lambda *_: zeros, pipeline_mode=pl.Buffered(1))


def _ada_kernel(c_ref, w_ref, b_ref, o_ref):
    c = c_ref[...]
    s = (c * _sigmoid(c)).astype(BF16)
    o_ref[0] = _dot(s, w_ref[0].astype(BF16)) + b_ref[0]


def _ada_call(c_all, w_ada, b_ada):
    depth, d, n = w_ada.shape
    rows = c_all.shape[0]
    tn = n // 4
    return pl.pallas_call(
        _ada_kernel,
        grid=(depth, n // tn),
        in_specs=[
            pl.BlockSpec((rows, d), lambda l, j: (0, 0)),
            pl.BlockSpec((1, d, tn), lambda l, j: (l, 0, j)),
            pl.BlockSpec((1, 1, tn), lambda l, j: (l, 0, j)),
        ],
        out_specs=pl.BlockSpec((1, rows, tn), lambda l, j: (l, 0, j)),
        out_shape=jax.ShapeDtypeStruct((depth, rows, n), F32),
        compiler_params=pltpu.CompilerParams(
            dimension_semantics=("arbitrary", "arbitrary"), vmem_limit_bytes=VMEM_LIMIT),
        name="ada_mod",
    )(c_all, w_ada, b_ada.reshape(depth, 1, n))


def _tap_groups(base, ntaps, shift):
    by_residue = {}
    for j in range(ntaps):
        off = base + j * shift
        by_residue.setdefault(off % SUBLANES, []).append((j, off - off % SUBLANES))
    return sorted(by_residue.items())


def _conv_chunk(buf_ref, w_ref, tap_groups, c0, chunk):
    acc = None
    for res, taps in tap_groups:
        ext = chunk + (SUBLANES if res else 0)
        part = None
        for j, off in taps:
            term = buf_ref[pl.ds(c0 + off, ext), :] * w_ref[j:j + 1, :]
            part = term if part is None else part + term
        if res:
            part = part[res:res + chunk]
        acc = part if acc is None else acc + part
    return acc


def _mixin_kernel(x_ref, mod_ref, gpre_ref, win_ref, caw_ref, cab_ref, lng_ref, lnb_ref, cbw_ref,
                  goa_ref, gob_ref, sta_ref, stb_ref, *rest,
                  rows, shift, d_a, d_b, d_c, carry_a, carry_b, n_tiles, first_kept_tile, n_prev):
    prev_refs, rest = (rest[:2], rest[2:]) if n_prev else ((), rest)
    mixab_ref, q_ref, k_ref, v_ref, na_ref, nb_ref = rest[:6]
    abuf, ubuf = rest[-2:]
    t = pl.program_id(1)

    @pl.when(t == 0)
    def _():
        abuf[0:carry_a, :] = sta_ref[0]
        ubuf[0:carry_b, :] = stb_ref[0]

    x = x_ref[0]
    h = (_rms(x, gpre_ref[...]) * (1.0 + mod_ref[0, 1]) + mod_ref[0, 0]).astype(BF16)

    o_b = 2 * d_a
    o_c = o_b + 3 * d_b
    za = _dot(h, win_ref[:, 0:2 * d_a])
    abuf[carry_a:carry_a + rows, :] = za[:, 0:d_a] * _sigmoid(za[:, d_a:2 * d_a])
    zb = _dot(h, win_ref[:, o_b:o_b + 3 * d_b])
    ubuf[carry_b:carry_b + rows, :] = zb[:, 2 * d_b:3 * d_b] * zb[:, 0:d_b]
    b_gate = zb[:, d_b:2 * d_b]

    q_ref[0] = _dot(h, win_ref[:, o_c:o_c + d_c])
    k = _dot(h, win_ref[:, o_c + d_c:o_c + 2 * d_c])
    v = _dot(h, win_ref[:, o_c + 2 * d_c:o_c + 3 * d_c])
    k_ref[0] = k
    v_ref[0] = v

    taps_a = _tap_groups(carry_a - (CONV_A_WIDTH - 1) * shift, CONV_A_WIDTH, shift)
    chunk = min(rows, CONV_CHUNK_ROWS)
    a = jnp.concatenate([_conv_chunk(abuf, caw_ref, taps_a, c0, chunk) for c0 in range(0, rows, chunk)], axis=0)
    a = a + cab_ref[...]
    mu = jnp.mean(a, axis=-1, keepdims=True)
    ac = a - mu
    var = jnp.mean(ac * ac, axis=-1, keepdims=True)
    a = ac * lax.rsqrt(var + EPS) * lng_ref[...] + lnb_ref[...]
    a = a * _sigmoid(a)
    mixab_ref[0, :, 0:d_a] = _rms(a, goa_ref[...]).astype(BF16)
    na_ref[0] = abuf[rows:rows + carry_a, :]

    taps_b = _tap_groups(carry_b - (CONV_B_WIDTH - 1) * shift, CONV_B_WIDTH, shift)
    u = _conv_chunk(ubuf, cbw_ref, taps_b, 0, rows)
    mixab_ref[0, :, d_a:d_a + d_b] = _rms(b_gate * u, gob_ref[...]).astype(BF16)
    nb_ref[0] = ubuf[rows:rows + carry_b, :]

    if first_kept_tile is not None:
        kept_kt_ref, kept_vt_ref = rest[6:8]

        @pl.when(t >= first_kept_tile)
        def _():
            for idx, (kept_ref, new) in enumerate(((kept_kt_ref, k), (kept_vt_ref, v))):
                for i in range(n_prev):
                    kept_ref[i, 0] = prev_refs[idx][i, 0]
                kept_ref[n_prev, 0] = new.T

    if n_tiles > 1:
        abuf[0:carry_a, :] = abuf[rows:rows + carry_a, :]
        ubuf[0:carry_b, :] = ubuf[rows:rows + carry_b, :]


def _mixin_call(x, mod, st_a, st_b, lw, *, rows, shift, keep, prev_kept=()):
    groups, n, d = x.shape
    rm = mod.shape[2]
    d_a = lw['conv_a_w'].shape[1]
    d_b = lw['conv_b_w'].shape[1]
    d_c = lw['g_out_c'].shape[1]
    p_in = lw['w_in'].shape[1]
    carry_a, carry_b = st_a.shape[1], st_b.shape[1]
    n_tiles = n // rows
    assert keep % rows == 0 and keep <= n
    first_kept_tile = (n - keep) // rows if keep < n else None
    n_prev = prev_kept[0].shape[0] if prev_kept else 0
    kern = functools.partial(_mixin_kernel, rows=rows, shift=shift, d_a=d_a, d_b=d_b, d_c=d_c,
                             carry_a=carry_a, carry_b=carry_b, n_tiles=n_tiles, first_kept_tile=first_kept_tile,
                             n_prev=n_prev)
    kept_specs, kept_shapes, prev_specs = [], [], []
    if first_kept_tile is not None:
        kept_map = lambda g, t: (0, g, 0, jnp.maximum(t - first_kept_tile, 0))
        kept_specs = [pl.BlockSpec((n_prev + 1, 1, d_c, rows), kept_map)] * 2
        kept_shapes = [jax.ShapeDtypeStruct((n_prev + 1, groups, d_c, keep), F32)] * 2
        prev_specs = [pl.BlockSpec((n_prev, 1, d_c, rows), kept_map)] * (2 if n_prev else 0)
    tile = lambda w: pl.BlockSpec((1, rows, w), lambda g, t: (g, t, 0))
    per_group = lambda r, w: pl.BlockSpec((1, r, w), lambda g, t: (g, 0, 0))
    outs = pl.pallas_call(
        kern,
        grid=(groups, n_tiles),
        in_specs=[
            tile(d),
            pl.BlockSpec((1, 6, rm, d), lambda g, t: (g, 0, 0, 0)),
            _const_spec((1, d)),
            _const_spec((d, p_in)),
            _const_spec((CONV_A_WIDTH, d_a)),
            _const_spec((1, d_a)),
            _const_spec((1, d_a)),
            _const_spec((1, d_a)),
            _const_spec((CONV_B_WIDTH, d_b)),
            _const_spec((1, d_a)),
            _const_spec((1, d_b)),
            per_group(carry_a, d_a),
            per_group(carry_b, d_b),
        ] + prev_specs,
        out_specs=[
            tile(d_a + d_b), tile(d_c), tile(d_c), tile(d_c),
            per_group(carry_a, d_a), per_group(carry_b, d_b),
        ] + kept_specs,
        out_shape=[
            jax.ShapeDtypeStruct((groups, n, d_a + d_b), BF16),
            jax.ShapeDtypeStruct((groups, n, d_c), F32),
            jax.ShapeDtypeStruct((groups, n, d_c), F32),
            jax.ShapeDtypeStruct((groups, n, d_c), F32),
            jax.ShapeDtypeStruct((groups, carry_a, d_a), F32),
            jax.ShapeDtypeStruct((groups, carry_b, d_b), F32),
        ] + kept_shapes,
        scratch_shapes=[
            pltpu.VMEM((carry_a + rows, d_a), F32),
            pltpu.VMEM((carry_b + rows, d_b), F32),
        ],
        compiler_params=pltpu.CompilerParams(
            dimension_semantics=("arbitrary", "arbitrary"), vmem_limit_bytes=VMEM_LIMIT),
        name="mix_in",
    )(x, mod, lw['g_pre_mix'], lw['w_in'], lw['conv_a_w'], lw['conv_a_b'], lw['ln_a_g'], lw['ln_a_b'],
      lw['conv_b_w'], lw['g_out_a'], lw['g_out_b'], st_a, st_b, *prev_kept)
    return outs


def _attn_kernel(q_ref, k_ref, v_ref, o_ref, qs0, qs1, kts, vs, acc, mrun, lrun, bias, pbuf, mbuf,
                 qmid, kmid, vmid, *, seq):
    n_blk = seq // Q_BLK
    lane = lax.broadcasted_iota(jnp.int32, (1, LANES), 1)
    head0 = lane < HEAD_DIM
    scale = HEAD_DIM ** -0.5 * LOG2_E

    row = lax.broadcasted_iota(jnp.int32, (Q_BLK, 2 * Q_BLK), 0)
    col = lax.broadcasted_iota(jnp.int32, (Q_BLK, 2 * Q_BLK), 1)
    band = jnp.logical_and(col >= row, col <= row + Q_BLK)
    bias[0] = jnp.where(band, 0.0, NEG)
    bias[1] = jnp.where(jnp.logical_and(band, col >= Q_BLK), 0.0, NEG)
    vs[0:Q_BLK, 0:LANES] = jnp.zeros((Q_BLK, LANES), BF16)
    vs[:, LANES:2 * LANES] = jnp.ones((seq + Q_BLK, LANES), BF16)
    kts[0] = jnp.zeros((LANES, Q_BLK), BF16)

    mid = DILATED_PATTERNS[1][1]
    blk_per_mid = n_blk // mid
    staged = ((q_ref.at[0], qmid), (k_ref.at[0], kmid), (v_ref.at[0], vmid))

    def stage(jb, carry):
        phase = jb // blk_per_mid
        src = pl.ds(phase + (jb - phase * blk_per_mid) * (Q_BLK * mid), Q_BLK, stride=mid)
        dst = pl.ds(pl.multiple_of(jb * Q_BLK, Q_BLK), Q_BLK)
        for ref, buf in staged:
            buf[dst, :] = ref[src, :]
        return carry

    lax.fori_loop(0, n_blk, stage, 0, unroll=ATTN_UNROLL)

    order = DILATED_PATTERNS[::-1]
    n_grp = n_blk // ATTN_UNROLL
    for step, (_, dil) in enumerate(order):
        blk_per_phase = n_blk // dil

        def rows_of(j, dil=dil, blk_per_phase=blk_per_phase):
            if dil == 1:
                return pl.ds(pl.multiple_of(j * Q_BLK, Q_BLK), Q_BLK)
            phase = j // blk_per_phase
            i = j - phase * blk_per_phase
            return pl.ds(phase + i * (Q_BLK * dil), Q_BLK, stride=dil)

        def source_rows(j, dil=dil, blk_per_phase=blk_per_phase):
            contiguous = pl.ds(pl.multiple_of(j * Q_BLK, Q_BLK), Q_BLK)
            if dil == 1:
                return (q_ref.at[0], k_ref.at[0], v_ref.at[0]), contiguous
            if dil == mid:
                return (qmid, kmid, vmid), contiguous
            ratio = dil // mid
            phase = j // blk_per_phase
            start = (phase % mid) * (seq // mid) + phase // mid + (j - phase * blk_per_phase) * (Q_BLK * ratio)
            return (qmid, kmid, vmid), pl.ds(start, Q_BLK, stride=ratio)

        def split(g, source_rows=source_rows):
            g = jnp.minimum(g, n_grp - 1)
            for u in range(ATTN_UNROLL):
                j = g * ATTN_UNROLL + u
                (q_src, k_src, v_src), src = source_rows(j)
                dst = pl.ds(pl.multiple_of(j * Q_BLK, Q_BLK), Q_BLK)
                dst_kv = pl.ds(pl.multiple_of((j + 1) * Q_BLK, Q_BLK), Q_BLK)
                qv = q_src[src, :] * scale
                qs0[dst, :] = jnp.where(head0, qv, 0.0).astype(BF16)
                qs1[dst, :] = jnp.where(head0, 0.0, qv).astype(BF16)
                kts[j + 1] = k_src[src, :].T.astype(BF16)
                vs[dst_kv, 0:LANES] = v_src[src, :].astype(BF16)

        def scores(g, slot, blk_per_phase=blk_per_phase):
            g = jnp.minimum(g, n_grp - 1)
            for u in range(ATTN_UNROLL):
                j = g * ATTN_UNROLL + u
                qrows = pl.ds(pl.multiple_of(j * Q_BLK, Q_BLK), Q_BLK)
                kbt = jnp.concatenate([kts[j], kts[j + 1]], axis=1)
                bb = bias[(j % blk_per_phase == 0).astype(jnp.int32)]
                ms = []
                for h, qs in enumerate((qs0, qs1)):
                    s = _dot(qs[qrows, :], kbt) + bb
                    m = jnp.max(s, axis=-1, keepdims=True)
                    pbuf[slot, u, h] = jnp.exp2(s - m).astype(BF16)
                    ms.append(m)
                mbuf[slot, u] = jnp.where(head0, ms[0], ms[1])

        def values(g, slot, step=step, rows_of=rows_of):
            for u in range(ATTN_UNROLL):
                j = g * ATTN_UNROLL + u
                vb = vs[pl.ds(pl.multiple_of(j * Q_BLK, Q_BLK), 2 * Q_BLK), :]
                pv0 = _dot(pbuf[slot, u, 0], vb)
                pv1 = _dot(pbuf[slot, u, 1], vb)
                a_new = jnp.where(head0, pv0[:, :LANES], pv1[:, :LANES])
                l_new = jnp.where(head0, pv0[:, LANES:], pv1[:, LANES:])
                m_new = mbuf[slot, u]
                dst = rows_of(j)
                if step > 0:
                    m_old = mrun[dst, :]
                    m_tot = jnp.maximum(m_old, m_new)
                    w_old = jnp.exp2(m_old - m_tot)
                    w_new = jnp.exp2(m_new - m_tot)
                    a_new = w_old * acc[dst, :] + w_new * a_new
                    l_new = w_old * lrun[dst, :] + w_new * l_new
                    m_new = m_tot
                if step < len(order) - 1:
                    mrun[dst, :] = m_new
                    lrun[dst, :] = l_new
                    acc[dst, :] = a_new
                else:
                    o_ref.at[0][dst, :] = a_new / l_new

        for g in range(3):
            split(jnp.int32(g))
        scores(jnp.int32(0), 0)

        def pair(t, carry, split=split, scores=scores, values=values):
            g = 2 * t
            scores(g + 1, 1)
            values(g, 0)
            scores(g + 2, 0)
            values(g + 1, 1)
            split(g + 3)
            split(g + 4)
            return carry

        lax.fori_loop(0, n_grp // 2, pair, 0)


def _attn_call(q, k, v):
    bsz, seq, d_c = q.shape
    assert seq % (Q_BLK * DILATED_PATTERNS[-1][1]) == 0 and seq % (2 * ATTN_UNROLL * Q_BLK) == 0
    assert len(DILATED_PATTERNS) == 3 and DILATED_PATTERNS[0][1] == 1
    assert DILATED_PATTERNS[2][1] % DILATED_PATTERNS[1][1] == 0
    spec = pl.BlockSpec((1, seq, LANES), lambda b, h: (b, 0, h))
    return pl.pallas_call(
        functools.partial(_attn_kernel, seq=seq),
        grid=(bsz, d_c // LANES),
        in_specs=[spec, spec, spec],
        out_specs=spec,
        out_shape=jax.ShapeDtypeStruct((bsz, seq, d_c), F32),
        scratch_shapes=[
            pltpu.VMEM((seq, LANES), BF16),
            pltpu.VMEM((seq, LANES), BF16),
            pltpu.VMEM((seq // Q_BLK + 1, LANES, Q_BLK), BF16),
            pltpu.VMEM((seq + Q_BLK, 2 * LANES), BF16),
            pltpu.VMEM((seq, LANES), F32),
            pltpu.VMEM((seq, LANES), F32),
            pltpu.VMEM((seq, LANES), F32),
            pltpu.VMEM((2, Q_BLK, 2 * Q_BLK), F32),
            pltpu.VMEM((2, ATTN_UNROLL, 2, Q_BLK, 2 * Q_BLK), BF16),
            pltpu.VMEM((2, ATTN_UNROLL, Q_BLK, LANES), F32),
            pltpu.VMEM((seq, LANES), F32),
            pltpu.VMEM((seq, LANES), F32),
            pltpu.VMEM((seq, LANES), F32),
        ],
        compiler_params=pltpu.CompilerParams(
            dimension_semantics=("arbitrary", "arbitrary"), vmem_limit_bytes=VMEM_LIMIT),
        name="attn_prompt",
    )(q, k, v)


def _multiplicity(dist):
    cnt = jnp.zeros(dist.shape, F32)
    for window, dil in DILATED_PATTERNS:
        hit = jnp.logical_and(dist >= 0, jnp.logical_and(dist <= window, dist % dil == 0))
        cnt = cnt + jnp.where(hit, 1.0, 0.0)
    return cnt


def _attn_sample_kernel(q_ref, kn_ref, vn_ref, kt_ref, vt_ref, o_ref, cnt_c, cnt_n, *, n_q, n_heads, w_buf):
    @pl.when(pl.program_id(0) == 0)
    def _():
        def table(n_keys, first_pos):
            q_pos = w_buf + lax.broadcasted_iota(jnp.int32, (n_q, n_keys), 0)
            k_pos = first_pos + lax.broadcasted_iota(jnp.int32, (n_q, n_keys), 1)
            return _multiplicity(q_pos - k_pos)

        cnt_c[...] = table(w_buf, 0)
        cnt_n[...] = table(n_q, w_buf)

    mult_c = cnt_c[...]
    mult_n = cnt_n[...]
    for h in range(n_heads):
        q = (q_ref[0, h] * HEAD_DIM ** -0.5).astype(BF16)
        s_c = jnp.where(mult_c > 0.0, _dot(q, kt_ref[0, 0, h].astype(BF16)), NEG)
        s_n = jnp.where(mult_n > 0.0, _dot_nt(q, kn_ref[0, h].astype(BF16)), NEG)
        m = jnp.maximum(jnp.max(s_c, axis=-1, keepdims=True), jnp.max(s_n, axis=-1, keepdims=True))
        p_c = mult_c * jnp.exp(s_c - m)
        p_n = mult_n * jnp.exp(s_n - m)
        l = jnp.sum(p_c, axis=-1, keepdims=True) + jnp.sum(p_n, axis=-1, keepdims=True)
        o = _dot_nt(p_c.astype(BF16), vt_ref[0, 0, h].astype(BF16)) + _dot(p_n.astype(BF16), vn_ref[0, h].astype(BF16))
        o_ref[0, h] = o / l


def _attn_sample_call(q, k_new, v_new, cache_kt, cache_vt, layer):
    bsz, n_heads, n_q, head_dim = q.shape
    w_buf = cache_kt.shape[4]
    new_spec = pl.BlockSpec((1, n_heads, n_q, head_dim), lambda b: (b, 0, 0, 0))
    cache_spec = pl.BlockSpec((1, 1, n_heads, head_dim, w_buf), lambda b: (layer, b, 0, 0, 0))
    return pl.pallas_call(
        functools.partial(_attn_sample_kernel, n_q=n_q, n_heads=n_heads, w_buf=w_buf),
        grid=(bsz,),
        in_specs=[new_spec, new_spec, new_spec, cache_spec, cache_spec],
        out_specs=new_spec,
        out_shape=jax.ShapeDtypeStruct(q.shape, F32),
        scratch_shapes=[pltpu.VMEM((n_q, w_buf), F32), pltpu.VMEM((n_q, n_q), F32)],
        compiler_params=pltpu.CompilerParams(
            dimension_semantics=("arbitrary",), vmem_limit_bytes=VMEM_LIMIT),
        name="attn_sample",
    )(q, k_new, v_new, cache_kt, cache_vt)


def _outffn_kernel(mixab_ref, o_ref, x_ref, mod_ref, goc_ref, wo_ref, gpm_ref, gpf_ref, wg_ref, wu_ref,
                   cfw_ref, wd_ref, gpo_ref, stf_ref,
                   y_ref, nf_ref, gbuf, *, rows, shift, carry_f, n_tiles):
    t = pl.program_id(1)

    @pl.when(t == 0)
    def _():
        gbuf[0:carry_f, :] = stf_ref[0]

    d_ab = mixab_ref.shape[2]
    oc = _rms(o_ref[0], goc_ref[...]).astype(BF16)
    y = _dot(mixab_ref[0], wo_ref[0:d_ab, :]) + _dot(oc, wo_ref[d_ab:, :])
    x1 = x_ref[0] + mod_ref[0, 2] * _rms(y, gpm_ref[...])

    h = (_rms(x1, gpf_ref[...]) * (1.0 + mod_ref[0, 4]) + mod_ref[0, 3]).astype(BF16)
    gbuf[carry_f:carry_f + rows, :] = _dot(h, wg_ref[...])
    taps = _tap_groups(carry_f - (FFN_CONV_WIDTH - 1) * shift, FFN_CONV_WIDTH, shift)
    g = _conv_chunk(gbuf, cfw_ref, taps, 0, rows)
    f = (g * _sigmoid(g) * _dot(h, wu_ref[...])).astype(BF16)
    y2 = _dot(f, wd_ref[...])
    y_ref[0] = x1 + mod_ref[0, 5] * _rms(y2, gpo_ref[...])
    nf_ref[0] = gbuf[rows:rows + carry_f, :]
    if n_tiles > 1:
        gbuf[0:carry_f, :] = gbuf[rows:rows + carry_f, :]


def _outffn_call(mixab, o, x, mod, st_f, lw, *, rows, shift):
    groups, n, d = x.shape
    rm = mod.shape[2]
    d_ab = mixab.shape[2]
    d_c = o.shape[2]
    d_ff = lw['w_gate'].shape[1]
    carry_f = st_f.shape[1]
    n_tiles = n // rows
    kern = functools.partial(_outffn_kernel, rows=rows, shift=shift, carry_f=carry_f, n_tiles=n_tiles)
    tile = lambda w: pl.BlockSpec((1, rows, w), lambda g, t: (g, t, 0))
    per_group = lambda r, w: pl.BlockSpec((1, r, w), lambda g, t: (g, 0, 0))
    return pl.pallas_call(
        kern,
        grid=(groups, n_tiles),
        in_specs=[
            tile(d_ab), tile(d_c), tile(d),
            pl.BlockSpec((1, 6, rm, d), lambda g, t: (g, 0, 0, 0)),
            _const_spec((1, d_c)),
            _const_spec((d_ab + d_c, d)),
            _const_spec((1, d)),
            _const_spec((1, d)),
            _const_spec((d, d_ff)),
            _const_spec((d, d_ff)),
            _const_spec((FFN_CONV_WIDTH, d_ff)),
            _const_spec((d_ff, d)),
            _const_spec((1, d)),
            per_group(carry_f, d_ff),
        ],
        out_specs=[tile(d), per_group(carry_f, d_ff)],
        out_shape=[
            jax.ShapeDtypeStruct((groups, n, d), F32),
            jax.ShapeDtypeStruct((groups, carry_f, d_ff), F32),
        ],
        scratch_shapes=[pltpu.VMEM((carry_f + rows, d_ff), F32)],
        compiler_params=pltpu.CompilerParams(
            dimension_semantics=("arbitrary", "arbitrary"), vmem_limit_bytes=VMEM_LIMIT),
        name="out_ffn",
    )(mixab, o, x, mod, lw['g_out_c'], lw['w_o'], lw['g_post_mix'], lw['g_pre_ffn'], lw['w_gate'],
      lw['w_up'], lw['conv_f_w'], lw['w_down'], lw['g_post_ffn'], st_f)


def _front_pad(state, rows):
    return jnp.pad(state, ((0, 0), (rows - state.shape[1], 0), (0, 0)))


def _prompt_layer(x, mod, lw, prev_kept):
    bsz, seq, d = x.shape
    d_a, d_b = lw['conv_a_w'].shape[1], lw['conv_b_w'].shape[1]
    d_ff = lw['w_gate'].shape[1]
    rows = PROMPT_TILE_ROWS if seq % PROMPT_TILE_ROWS == 0 else seq
    carry_a = _round_up(CONV_A_WIDTH - 1, SUBLANES)
    carry_b = _round_up(CONV_B_WIDTH - 1, SUBLANES)
    carry_f = _round_up(FFN_CONV_WIDTH - 1, SUBLANES)
    mod4 = mod.reshape(bsz, 6, 1, d)
    keep = min(DILATED_PATTERNS[-1][0], seq)
    assert keep < seq
    mixab, q, k, v, na, nb, kept_kt, kept_vt = _mixin_call(
        x, mod4, jnp.zeros((bsz, carry_a, d_a), F32), jnp.zeros((bsz, carry_b, d_b), F32), lw,
        rows=rows, shift=1, keep=keep, prev_kept=prev_kept)
    o = _attn_call(q, k, v)
    y, nf = _outffn_call(mixab, o, x, mod4, jnp.zeros((bsz, carry_f, d_ff), F32), lw, rows=rows, shift=1)
    return (y, (kept_kt, kept_vt), na[:, carry_a - (CONV_A_WIDTH - 1):], nb[:, carry_b - (CONV_B_WIDTH - 1):],
            nf[:, carry_f - (FFN_CONV_WIDTH - 1):])


def _time_major(state):
    bsz, k, c = state.shape
    return state.transpose(1, 0, 2).reshape(1, k * bsz, c)


def _batch_major(rows, bsz):
    _, n, c = rows.shape
    return rows.reshape(n // bsz, bsz, c).transpose(1, 0, 2)


def _sample_layer(x_tm, mod_tm, st_a, st_b, st_f, cache_kt, cache_vt, layer, lw, bsz):
    n = x_tm.shape[1]
    carry_a = _round_up((CONV_A_WIDTH - 1) * bsz, SUBLANES)
    carry_b = _round_up((CONV_B_WIDTH - 1) * bsz, SUBLANES)
    carry_f = _round_up((FFN_CONV_WIDTH - 1) * bsz, SUBLANES)
    mixab, q, k, v, na, nb = _mixin_call(
        x_tm, mod_tm, _front_pad(_time_major(st_a), carry_a), _front_pad(_time_major(st_b), carry_b), lw,
        rows=n, shift=bsz, keep=n)
    qb, kb, vb = (_batch_major(a, bsz) for a in (q, k, v))
    n_q, n_heads = qb.shape[1], qb.shape[2] // HEAD_DIM
    per_head = lambda a: a.reshape(bsz, n_q, n_heads, HEAD_DIM).transpose(0, 2, 1, 3)
    o = _attn_sample_call(per_head(qb), per_head(kb), per_head(vb), cache_kt, cache_vt, layer)
    o_tm = o.transpose(2, 0, 1, 3).reshape(1, n, n_heads * HEAD_DIM)
    y, nf = _outffn_call(mixab, o_tm, x_tm, mod_tm, _front_pad(_time_major(st_f), carry_f), lw,
                         rows=n, shift=bsz)
    new_k = kb.reshape(bsz, kb.shape[1], n_heads, HEAD_DIM)
    new_v = vb.reshape(bsz, vb.shape[1], n_heads, HEAD_DIM)
    new_a = _batch_major(na[:, carry_a - (CONV_A_WIDTH - 1) * bsz:], bsz)
    new_b = _batch_major(nb[:, carry_b - (CONV_B_WIDTH - 1) * bsz:], bsz)
    new_f = _batch_major(nf[:, carry_f - (FFN_CONV_WIDTH - 1) * bsz:], bsz)
    return y, new_k, new_v, new_a, new_b, new_f


def kernel(x_prompt, x_sample, cache_k, cache_v, state_conv_a, state_conv_b, state_ffn_conv, c_prompt, c_sample, w_ada, b_ada, g_pre_mix, w_in, conv_a_w, conv_a_b, ln_a_g, ln_a_b, conv_b_w, g_out_a, g_out_b, g_out_c, w_o, g_post_mix, g_pre_ffn, w_gate, w_up, conv_f_w, w_down, g_post_ffn):
    depth = w_ada.shape[0]
    bsz_p, _, d = x_prompt.shape
    bsz_s, t_s, _ = x_sample.shape

    mod = _ada_call(jnp.concatenate([c_prompt, c_sample], axis=0), w_ada, b_ada)

    cache_kt = cache_k.transpose(0, 1, 3, 4, 2)
    cache_vt = cache_v.transpose(0, 1, 3, 4, 2)

    xp = x_prompt
    xs = x_sample.transpose(1, 0, 2).reshape(1, t_s * bsz_s, d)
    outs = [[] for _ in range(8)]
    kept = ()
    for l in range(depth):
        row = lambda a: a[l][None, :]
        lw = {
            'g_pre_mix': row(g_pre_mix), 'w_in': w_in[l].astype(BF16),
            'conv_a_w': conv_a_w[l], 'conv_a_b': row(conv_a_b), 'ln_a_g': row(ln_a_g), 'ln_a_b': row(ln_a_b),
            'conv_b_w': conv_b_w[l], 'g_out_a': row(g_out_a), 'g_out_b': row(g_out_b), 'g_out_c': row(g_out_c),
            'w_o': w_o[l].astype(BF16), 'g_post_mix': row(g_post_mix), 'g_pre_ffn': row(g_pre_ffn),
            'w_gate': w_gate[l].astype(BF16), 'w_up': w_up[l].astype(BF16), 'conv_f_w': conv_f_w[l],
            'w_down': w_down[l].astype(BF16), 'g_post_ffn': row(g_post_ffn),
        }
        mod_p = mod[l, :bsz_p].reshape(bsz_p, 6, d)
        mod_s = mod[l, bsz_p:].reshape(bsz_s, 6, d).transpose(1, 0, 2)
        mod_s = jnp.tile(mod_s[:, None], (1, t_s, 1, 1)).reshape(1, 6, t_s * bsz_s, d)

        xp, kept, ap, bp, fp = _prompt_layer(xp, mod_p, lw, kept)
        xs, ks_, vs_, as_, bs_, fs_ = _sample_layer(
            xs, mod_s, state_conv_a[l], state_conv_b[l], state_ffn_conv[l], cache_kt, cache_vt, l, lw, bsz_s)
        for lst, val in zip(outs, (ks_, vs_, ap, as_, bp, bs_, fp, fs_)):
            lst.append(val)
    ys = xs.reshape(t_s, bsz_s, d).transpose(1, 0, 2)
    kp, vp = (a.reshape(a.shape[:2] + (-1, HEAD_DIM, a.shape[3])).transpose(0, 1, 4, 2, 3) for a in kept)
    return (xp, ys, kp, vp) + tuple(jnp.stack(o) for o in outs)
```

```python
import functools

import jax
import jax.numpy as jnp
from jax import lax
from jax.experimental import pallas as pl
from jax.experimental.pallas import tpu as pltpu

F32 = jnp.float32
BF16 = jnp.bfloat16

HEAD_DIM = 64
LANES = 128
SUBLANES = 8
CONV_A_WIDTH = 31
CONV_B_WIDTH = 3
FFN_CONV_WIDTH = 3
DILATED_PATTERNS = ((128, 1), (512, 4), (2048, 16))
Q_BLK = 128
EPS = 1e-6
NEG = -1e30
LOG2_E = 1.4426950408889634
ATTN_UNROLL = 4
VMEM_LIMIT = 56 * 1024 * 1024
PROMPT_TILE_ROWS = 512
CONV_CHUNK_ROWS = 64


def _round_up(n, m):
    return (n + m - 1) // m * m


def _rms(x, g):
    return x * lax.rsqrt(jnp.mean(x * x, axis=-1, keepdims=True) + EPS) * g


def _sigmoid(x):
    return 1.0 / (1.0 + jnp.exp(-x))


def _dot(a, b):
    return jnp.dot(a, b, preferred_element_type=F32)


def _dot_nt(a, b):
    return lax.dot_general(a, b, (((1,), (1,)), ((), ())), preferred_element_type=F32)


def _const_spec(shape):
    zeros = (0,) * len(shape)
    return pl.BlockSpec(shape, lambda *_: zeros, pipeline_mode=pl.Buffered(1))


def _ada_kernel(c_ref, w_ref, b_ref, o_ref):
    c = c_ref[...]
    s = (c * _sigmoid(c)).astype(BF16)
    o_ref[0] = _dot(s, w_ref[0].astype(BF16)) + b_ref[0]


def _ada_call(c_all, w_ada, b_ada):
    depth, d, n = w_ada.shape
    rows = c_all.shape[0]
    tn = n // 4
    return pl.pallas_call(
        _ada_kernel,
        grid=(depth, n // tn),
        in_specs=[
            pl.BlockSpec((rows, d), lambda l, j: (0, 0)),
            pl.BlockSpec((1, d, tn), lambda l, j: (l, 0, j)),
            pl.BlockSpec((1, 1, tn), lambda l, j: (l, 0, j)),
        ],
        out_specs=pl.BlockSpec((1, rows, tn), lambda l, j: (l, 0, j)),
        out_shape=jax.ShapeDtypeStruct((depth, rows, n), F32),
        compiler_params=pltpu.CompilerParams(
            dimension_semantics=("arbitrary", "arbitrary"), vmem_limit_bytes=VMEM_LIMIT),
        name="ada_mod",
    )(c_all, w_ada, b_ada.reshape(depth, 1, n))


def _tap_groups(base, ntaps, shift):
    by_residue = {}
    for j in range(ntaps):
        off = base + j * shift
        by_residue.setdefault(off % SUBLANES, []).append((j, off - off % SUBLANES))
    return sorted(by_residue.items())


def _conv_chunk(buf_ref, w_ref, tap_groups, c0, chunk):
    acc = None
    for res, taps in tap_groups:
        ext = chunk + (SUBLANES if res else 0)
        part = None
        for j, off in taps:
            term = buf_ref[pl.ds(c0 + off, ext), :] * w_ref[j:j + 1, :]
            part = term if part is None else part + term
        if res:
            part = part[res:res + chunk]
        acc = part if acc is None else acc + part
    return acc


def _mixin_kernel(x_ref, mod_ref, gpre_ref, win_ref, caw_ref, cab_ref, lng_ref, lnb_ref, cbw_ref,
                  goa_ref, gob_ref, sta_ref, stb_ref, *rest,
                  rows, shift, d_a, d_b, d_c, carry_a, carry_b, n_tiles, first_kept_tile, n_prev):
    prev_refs, rest = (rest[:2], rest[2:]) if n_prev else ((), rest)
    mixab_ref, q_ref, k_ref, v_ref, na_ref, nb_ref = rest[:6]
    abuf, ubuf = rest[-2:]
    t = pl.program_id(1)

    @pl.when(t == 0)
    def _():
        abuf[0:carry_a, :] = sta_ref[0]
        ubuf[0:carry_b, :] = stb_ref[0]

    x = x_ref[0]
    h = (_rms(x, gpre_ref[...]) * (1.0 + mod_ref[0, 1]) + mod_ref[0, 0]).astype(BF16)

    o_b = 2 * d_a
    o_c = o_b + 3 * d_b
    za = _dot(h, win_ref[:, 0:2 * d_a])
    abuf[carry_a:carry_a + rows, :] = za[:, 0:d_a] * _sigmoid(za[:, d_a:2 * d_a])
    zb = _dot(h, win_ref[:, o_b:o_b + 3 * d_b])
    ubuf[carry_b:carry_b + rows, :] = zb[:, 2 * d_b:3 * d_b] * zb[:, 0:d_b]
    b_gate = zb[:, d_b:2 * d_b]

    q = _dot(h, win_ref[:, o_c:o_c + d_c])
    k = _dot(h, win_ref[:, o_c + d_c:o_c + 2 * d_c])
    v = _dot(h, win_ref[:, o_c + 2 * d_c:o_c + 3 * d_c])
    for ref, val in ((q_ref, q), (k_ref, k), (v_ref, v)):
        for grp in range(d_c // LANES):
            ref[0, grp] = val[:, grp * LANES:(grp + 1) * LANES]

    taps_a = _tap_groups(carry_a - (CONV_A_WIDTH - 1) * shift, CONV_A_WIDTH, shift)
    chunk = min(rows, CONV_CHUNK_ROWS)
    a = jnp.concatenate([_conv_chunk(abuf, caw_ref, taps_a, c0, chunk) for c0 in range(0, rows, chunk)], axis=0)
    a = a + cab_ref[...]
    mu = jnp.mean(a, axis=-1, keepdims=True)
    ac = a - mu
    var = jnp.mean(ac * ac, axis=-1, keepdims=True)
    a = ac * lax.rsqrt(var + EPS) * lng_ref[...] + lnb_ref[...]
    a = a * _sigmoid(a)
    mixab_ref[0, :, 0:d_a] = _rms(a, goa_ref[...]).astype(BF16)
    na_ref[0] = abuf[rows:rows + carry_a, :]

    taps_b = _tap_groups(carry_b - (CONV_B_WIDTH - 1) * shift, CONV_B_WIDTH, shift)
    u = _conv_chunk(ubuf, cbw_ref, taps_b, 0, rows)
    mixab_ref[0, :, d_a:d_a + d_b] = _rms(b_gate * u, gob_ref[...]).astype(BF16)
    nb_ref[0] = ubuf[rows:rows + carry_b, :]

    if first_kept_tile is not None:
        kept_kt_ref, kept_vt_ref = rest[6:8]

        @pl.when(t >= first_kept_tile)
        def _():
            for idx, (kept_ref, new) in enumerate(((kept_kt_ref, k), (kept_vt_ref, v))):
                for i in range(n_prev):
                    kept_ref[i, 0] = prev_refs[idx][i, 0]
                kept_ref[n_prev, 0] = new.T

    if n_tiles > 1:
        abuf[0:carry_a, :] = abuf[rows:rows + carry_a, :]
        ubuf[0:carry_b, :] = ubuf[rows:rows + carry_b, :]


def _mixin_call(x, mod, st_a, st_b, lw, *, rows, shift, keep, prev_kept=()):
    groups, n, d = x.shape
    rm = mod.shape[2]
    d_a = lw['conv_a_w'].shape[1]
    d_b = lw['conv_b_w'].shape[1]
    d_c = lw['g_out_c'].shape[1]
    p_in = lw['w_in'].shape[1]
    carry_a, carry_b = st_a.shape[1], st_b.shape[1]
    n_tiles = n // rows
    assert keep % rows == 0 and keep <= n
    first_kept_tile = (n - keep) // rows if keep < n else None
    n_prev = prev_kept[0].shape[0] if prev_kept else 0
    kern = functools.partial(_mixin_kernel, rows=rows, shift=shift, d_a=d_a, d_b=d_b, d_c=d_c,
                             carry_a=carry_a, carry_b=carry_b, n_tiles=n_tiles, first_kept_tile=first_kept_tile,
                             n_prev=n_prev)
    kept_specs, kept_shapes, prev_specs = [], [], []
    if first_kept_tile is not None:
        kept_map = lambda g, t: (0, g, 0, jnp.maximum(t - first_kept_tile, 0))
        kept_specs = [pl.BlockSpec((n_prev + 1, 1, d_c, rows), kept_map)] * 2
        kept_shapes = [jax.ShapeDtypeStruct((n_prev + 1, groups, d_c, keep), F32)] * 2
        prev_specs = [pl.BlockSpec((n_prev, 1, d_c, rows), kept_map)] * (2 if n_prev else 0)
    tile = lambda w: pl.BlockSpec((1, rows, w), lambda g, t: (g, t, 0))
    per_group = lambda r, w: pl.BlockSpec((1, r, w), lambda g, t: (g, 0, 0))
    grouped = pl.BlockSpec((1, d_c // LANES, rows, LANES), lambda g, t: (g, 0, t, 0))
    outs = pl.pallas_call(
        kern,
        grid=(groups, n_tiles),
        in_specs=[
            tile(d),
            pl.BlockSpec((1, 6, rm, d), lambda g, t: (g, 0, 0, 0)),
            _const_spec((1, d)),
            _const_spec((d, p_in)),
            _const_spec((CONV_A_WIDTH, d_a)),
            _const_spec((1, d_a)),
            _const_spec((1, d_a)),
            _const_spec((1, d_a)),
            _const_spec((CONV_B_WIDTH, d_b)),
            _const_spec((1, d_a)),
            _const_spec((1, d_b)),
            per_group(carry_a, d_a),
            per_group(carry_b, d_b),
        ] + prev_specs,
        out_specs=[
            tile(d_a + d_b), grouped, grouped, grouped,
            per_group(carry_a, d_a), per_group(carry_b, d_b),
        ] + kept_specs,
        out_shape=[
            jax.ShapeDtypeStruct((groups, n, d_a + d_b), BF16),
            jax.ShapeDtypeStruct((groups, d_c // LANES, n, LANES), F32),
            jax.ShapeDtypeStruct((groups, d_c // LANES, n, LANES), F32),
            jax.ShapeDtypeStruct((groups, d_c // LANES, n, LANES), F32),
            jax.ShapeDtypeStruct((groups, carry_a, d_a), F32),
            jax.ShapeDtypeStruct((groups, carry_b, d_b), F32),
        ] + kept_shapes,
        scratch_shapes=[
            pltpu.VMEM((carry_a + rows, d_a), F32),
            pltpu.VMEM((carry_b + rows, d_b), F32),
        ],
        compiler_params=pltpu.CompilerParams(
            dimension_semantics=("arbitrary", "arbitrary"), vmem_limit_bytes=VMEM_LIMIT),
        name="mix_in",
    )(x, mod, lw['g_pre_mix'], lw['w_in'], lw['conv_a_w'], lw['conv_a_b'], lw['ln_a_g'], lw['ln_a_b'],
      lw['conv_b_w'], lw['g_out_a'], lw['g_out_b'], st_a, st_b, *prev_kept)
    return outs


def _attn_kernel(q_ref, k_ref, v_ref, o_ref, qs0, qs1, kts, vs, acc, mrun, lrun, bias, pbuf, mbuf,
                 qmid, kmid, vmid, *, seq):
    n_blk = seq // Q_BLK
    lane = lax.broadcasted_iota(jnp.int32, (1, LANES), 1)
    head0 = lane < HEAD_DIM
    scale = HEAD_DIM ** -0.5 * LOG2_E

    @pl.when(jnp.logical_and(pl.program_id(0) == 0, pl.program_id(1) == 0))
    def _():
        row = lax.broadcasted_iota(jnp.int32, (Q_BLK, 2 * Q_BLK), 0)
        col = lax.broadcasted_iota(jnp.int32, (Q_BLK, 2 * Q_BLK), 1)
        band = jnp.logical_and(col >= row, col <= row + Q_BLK)
        bias[0] = jnp.where(band, 0.0, NEG)
        bias[1] = jnp.where(jnp.logical_and(band, col >= Q_BLK), 0.0, NEG)
        vs[0:Q_BLK, 0:LANES] = jnp.zeros((Q_BLK, LANES), BF16)
        vs[:, LANES:2 * LANES] = jnp.ones((seq + Q_BLK, LANES), BF16)
        kts[0] = jnp.zeros((LANES, Q_BLK), BF16)

    mid = DILATED_PATTERNS[1][1]
    blk_per_mid = n_blk // mid
    staged = ((q_ref.at[0, 0], qmid), (k_ref.at[0, 0], kmid), (v_ref.at[0, 0], vmid))

    def stage(jb, carry):
        phase = jb // blk_per_mid
        src = pl.ds(phase + (jb - phase * blk_per_mid) * (Q_BLK * mid), Q_BLK, stride=mid)
        dst = pl.ds(pl.multiple_of(jb * Q_BLK, Q_BLK), Q_BLK)
        for ref, buf in staged:
            buf[dst, :] = ref[src, :]
        return carry

    lax.fori_loop(0, n_blk, stage, 0, unroll=ATTN_UNROLL)

    order = DILATED_PATTERNS[::-1]
    n_grp = n_blk // ATTN_UNROLL
    for step, (_, dil) in enumerate(order):
        blk_per_phase = n_blk // dil

        def rows_of(j, dil=dil, blk_per_phase=blk_per_phase):
            if dil == 1:
                return pl.ds(pl.multiple_of(j * Q_BLK, Q_BLK), Q_BLK)
            phase = j // blk_per_phase
            i = j - phase * blk_per_phase
            return pl.ds(phase + i * (Q_BLK * dil), Q_BLK, stride=dil)

        def source_rows(j, dil=dil, blk_per_phase=blk_per_phase):
            contiguous = pl.ds(pl.multiple_of(j * Q_BLK, Q_BLK), Q_BLK)
            if dil == 1:
                return (q_ref.at[0, 0], k_ref.at[0, 0], v_ref.at[0, 0]), contiguous
            if dil == mid:
                return (qmid, kmid, vmid), contiguous
            ratio = dil // mid
            phase = j // blk_per_phase
            start = (phase % mid) * (seq // mid) + phase // mid + (j - phase * blk_per_phase) * (Q_BLK * ratio)
            return (qmid, kmid, vmid), pl.ds(start, Q_BLK, stride=ratio)

        def split(g, source_rows=source_rows):
            g = jnp.minimum(g, n_grp - 1)
            for u in range(ATTN_UNROLL):
                j = g * ATTN_UNROLL + u
                (q_src, k_src, v_src), src = source_rows(j)
                dst = pl.ds(pl.multiple_of(j * Q_BLK, Q_BLK), Q_BLK)
                dst_kv = pl.ds(pl.multiple_of((j + 1) * Q_BLK, Q_BLK), Q_BLK)
                qv = q_src[src, :] * scale
                qs0[dst, :] = jnp.where(head0, qv, 0.0).astype(BF16)
                qs1[dst, :] = jnp.where(head0, 0.0, qv).astype(BF16)
                kts[j + 1] = k_src[src, :].T.astype(BF16)
                vs[dst_kv, 0:LANES] = v_src[src, :].astype(BF16)

        def scores(g, slot, blk_per_phase=blk_per_phase):
            g = jnp.minimum(g, n_grp - 1)
            for u in range(ATTN_UNROLL):
                j = g * ATTN_UNROLL + u
                qrows = pl.ds(pl.multiple_of(j * Q_BLK, Q_BLK), Q_BLK)
                kbt = jnp.concatenate([kts[j], kts[j + 1]], axis=1)
                bb = bias[(j % blk_per_phase == 0).astype(jnp.int32)]
                ms = []
                for h, qs in enumerate((qs0, qs1)):
                    s = _dot(qs[qrows, :], kbt) + bb
                    m = jnp.max(s, axis=-1, keepdims=True)
                    pbuf[slot, u, h] = jnp.exp2(s - m).astype(BF16)
                    ms.append(m)
                mbuf[slot, u] = jnp.where(head0, ms[0], ms[1])

        def values(g, slot, step=step, rows_of=rows_of):
            for u in range(ATTN_UNROLL):
                j = g * ATTN_UNROLL + u
                vb = vs[pl.ds(pl.multiple_of(j * Q_BLK, Q_BLK), 2 * Q_BLK), :]
                pv0 = _dot(pbuf[slot, u, 0], vb)
                pv1 = _dot(pbuf[slot, u, 1], vb)
                a_new = jnp.where(head0, pv0[:, :LANES], pv1[:, :LANES])
                l_new = jnp.where(head0, pv0[:, LANES:], pv1[:, LANES:])
                m_new = mbuf[slot, u]
                dst = rows_of(j)
                if step > 0:
                    m_old = mrun[dst, :]
                    m_tot = jnp.maximum(m_old, m_new)
                    w_old = jnp.exp2(m_old - m_tot)
                    w_new = jnp.exp2(m_new - m_tot)
                    a_new = w_old * acc[dst, :] + w_new * a_new
                    l_new = w_old * lrun[dst, :] + w_new * l_new
                    m_new = m_tot
                if step < len(order) - 1:
                    mrun[dst, :] = m_new
                    lrun[dst, :] = l_new
                    acc[dst, :] = a_new
                else:
                    o_ref.at[0, 0][dst, :] = a_new / l_new

        for g in range(3):
            split(jnp.int32(g))
        scores(jnp.int32(0), 0)

        def pair(t, carry, split=split, scores=scores, values=values):
            g = 2 * t
            scores(g + 1, 1)
            values(g, 0)
            scores(g + 2, 0)
            values(g + 1, 1)
            split(g + 3)
            split(g + 4)
            return carry

        lax.fori_loop(0, n_grp // 2, pair, 0)


def _attn_call(q, k, v):
    bsz, n_grp, seq, _ = q.shape
    assert seq % (Q_BLK * DILATED_PATTERNS[-1][1]) == 0 and seq % (2 * ATTN_UNROLL * Q_BLK) == 0
    assert len(DILATED_PATTERNS) == 3 and DILATED_PATTERNS[0][1] == 1
    assert DILATED_PATTERNS[2][1] % DILATED_PATTERNS[1][1] == 0
    spec = pl.BlockSpec((1, 1, seq, LANES), lambda b, h: (b, h, 0, 0))
    return pl.pallas_call(
        functools.partial(_attn_kernel, seq=seq),
        grid=(bsz, n_grp),
        in_specs=[spec, spec, spec],
        out_specs=spec,
        out_shape=jax.ShapeDtypeStruct(q.shape, F32),
        scratch_shapes=[
            pltpu.VMEM((seq, LANES), BF16),
            pltpu.VMEM((seq, LANES), BF16),
            pltpu.VMEM((seq // Q_BLK + 1, LANES, Q_BLK), BF16),
            pltpu.VMEM((seq + Q_BLK, 2 * LANES), BF16),
            pltpu.VMEM((seq, LANES), F32),
            pltpu.VMEM((seq, LANES), F32),
            pltpu.VMEM((seq, LANES), F32),
            pltpu.VMEM((2, Q_BLK, 2 * Q_BLK), F32),
            pltpu.VMEM((2, ATTN_UNROLL, 2, Q_BLK, 2 * Q_BLK), BF16),
            pltpu.VMEM((2, ATTN_UNROLL, Q_BLK, LANES), F32),
            pltpu.VMEM((seq, LANES), F32),
            pltpu.VMEM((seq, LANES), F32),
            pltpu.VMEM((seq, LANES), F32),
        ],
        compiler_params=pltpu.CompilerParams(
            dimension_semantics=("arbitrary", "arbitrary"), vmem_limit_bytes=VMEM_LIMIT),
        name="attn_prompt",
    )(q, k, v)


def _multiplicity(dist):
    cnt = jnp.zeros(dist.shape, F32)
    for window, dil in DILATED_PATTERNS:
        hit = jnp.logical_and(dist >= 0, jnp.logical_and(dist <= window, dist % dil == 0))
        cnt = cnt + jnp.where(hit, 1.0, 0.0)
    return cnt


def _attn_sample_kernel(q_ref, kn_ref, vn_ref, kt_ref, vt_ref, o_ref, cnt_c, cnt_n, *, n_q, n_heads, w_buf):
    @pl.when(pl.program_id(0) == 0)
    def _():
        def table(n_keys, first_pos):
            q_pos = w_buf + lax.broadcasted_iota(jnp.int32, (n_q, n_keys), 0)
            k_pos = first_pos + lax.broadcasted_iota(jnp.int32, (n_q, n_keys), 1)
            return _multiplicity(q_pos - k_pos)

        cnt_c[...] = table(w_buf, 0)
        cnt_n[...] = table(n_q, w_buf)

    mult_c = cnt_c[...]
    mult_n = cnt_n[...]
    for h in range(n_heads):
        q = (q_ref[0, h] * HEAD_DIM ** -0.5).astype(BF16)
        s_c = jnp.where(mult_c > 0.0, _dot(q, kt_ref[0, 0, h].astype(BF16)), NEG)
        s_n = jnp.where(mult_n > 0.0, _dot_nt(q, kn_ref[0, h].astype(BF16)), NEG)
        m = jnp.maximum(jnp.max(s_c, axis=-1, keepdims=True), jnp.max(s_n, axis=-1, keepdims=True))
        p_c = mult_c * jnp.exp(s_c - m)
        p_n = mult_n * jnp.exp(s_n - m)
        l = jnp.sum(p_c, axis=-1, keepdims=True) + jnp.sum(p_n, axis=-1, keepdims=True)
        o = _dot_nt(p_c.astype(BF16), vt_ref[0, 0, h].astype(BF16)) + _dot(p_n.astype(BF16), vn_ref[0, h].astype(BF16))
        o_ref[0, h] = o / l


def _attn_sample_call(q, k_new, v_new, cache_kt, cache_vt, layer):
    bsz, n_heads, n_q, head_dim = q.shape
    w_buf = cache_kt.shape[4]
    new_spec = pl.BlockSpec((1, n_heads, n_q, head_dim), lambda b: (b, 0, 0, 0))
    cache_spec = pl.BlockSpec((1, 1, n_heads, head_dim, w_buf), lambda b: (layer, b, 0, 0, 0))
    return pl.pallas_call(
        functools.partial(_attn_sample_kernel, n_q=n_q, n_heads=n_heads, w_buf=w_buf),
        grid=(bsz,),
        in_specs=[new_spec, new_spec, new_spec, cache_spec, cache_spec],
        out_specs=new_spec,
        out_shape=jax.ShapeDtypeStruct(q.shape, F32),
        scratch_shapes=[pltpu.VMEM((n_q, w_buf), F32), pltpu.VMEM((n_q, n_q), F32)],
        compiler_params=pltpu.CompilerParams(
            dimension_semantics=("arbitrary",), vmem_limit_bytes=VMEM_LIMIT),
        name="attn_sample",
    )(q, k_new, v_new, cache_kt, cache_vt)


def _outffn_kernel(mixab_ref, o_ref, x_ref, mod_ref, goc_ref, wo_ref, gpm_ref, gpf_ref, wg_ref, wu_ref,
                   cfw_ref, wd_ref, gpo_ref, stf_ref,
                   y_ref, nf_ref, gbuf, *, rows, shift, carry_f, n_tiles):
    t = pl.program_id(1)

    @pl.when(t == 0)
    def _():
        gbuf[0:carry_f, :] = stf_ref[0]

    d_ab = mixab_ref.shape[2]
    o = jnp.concatenate([o_ref[0, grp] for grp in range(o_ref.shape[1])], axis=-1)
    oc = _rms(o, goc_ref[...]).astype(BF16)
    y = _dot(mixab_ref[0], wo_ref[0:d_ab, :]) + _dot(oc, wo_ref[d_ab:, :])
    x1 = x_ref[0] + mod_ref[0, 2] * _rms(y, gpm_ref[...])

    h = (_rms(x1, gpf_ref[...]) * (1.0 + mod_ref[0, 4]) + mod_ref[0, 3]).astype(BF16)
    gbuf[carry_f:carry_f + rows, :] = _dot(h, wg_ref[...])
    taps = _tap_groups(carry_f - (FFN_CONV_WIDTH - 1) * shift, FFN_CONV_WIDTH, shift)
    g = _conv_chunk(gbuf, cfw_ref, taps, 0, rows)
    f = (g * _sigmoid(g) * _dot(h, wu_ref[...])).astype(BF16)
    y2 = _dot(f, wd_ref[...])
    y_ref[0] = x1 + mod_ref[0, 5] * _rms(y2, gpo_ref[...])
    nf_ref[0] = gbuf[rows:rows + carry_f, :]
    if n_tiles > 1:
        gbuf[0:carry_f, :] = gbuf[rows:rows + carry_f, :]


def _outffn_call(mixab, o, x, mod, st_f, lw, *, rows, shift):
    groups, n, d = x.shape
    rm = mod.shape[2]
    d_ab = mixab.shape[2]
    n_og = o.shape[1]
    d_c = n_og * LANES
    d_ff = lw['w_gate'].shape[1]
    carry_f = st_f.shape[1]
    n_tiles = n // rows
    kern = functools.partial(_outffn_kernel, rows=rows, shift=shift, carry_f=carry_f, n_tiles=n_tiles)
    tile = lambda w: pl.BlockSpec((1, rows, w), lambda g, t: (g, t, 0))
    per_group = lambda r, w: pl.BlockSpec((1, r, w), lambda g, t: (g, 0, 0))
    return pl.pallas_call(
        kern,
        grid=(groups, n_tiles),
        in_specs=[
            tile(d_ab), pl.BlockSpec((1, n_og, rows, LANES), lambda g, t: (g, 0, t, 0)), tile(d),
            pl.BlockSpec((1, 6, rm, d), lambda g, t: (g, 0, 0, 0)),
            _const_spec((1, d_c)),
            _const_spec((d_ab + d_c, d)),
            _const_spec((1, d)),
            _const_spec((1, d)),
            _const_spec((d, d_ff)),
            _const_spec((d, d_ff)),
            _const_spec((FFN_CONV_WIDTH, d_ff)),
            _const_spec((d_ff, d)),
            _const_spec((1, d)),
            per_group(carry_f, d_ff),
        ],
        out_specs=[tile(d), per_group(carry_f, d_ff)],
        out_shape=[
            jax.ShapeDtypeStruct((groups, n, d), F32),
            jax.ShapeDtypeStruct((groups, carry_f, d_ff), F32),
        ],
        scratch_shapes=[pltpu.VMEM((carry_f + rows, d_ff), F32)],
        compiler_params=pltpu.CompilerParams(
            dimension_semantics=("arbitrary", "arbitrary"), vmem_limit_bytes=VMEM_LIMIT),
        name="out_ffn",
    )(mixab, o, x, mod, lw['g_out_c'], lw['w_o'], lw['g_post_mix'], lw['g_pre_ffn'], lw['w_gate'],
      lw['w_up'], lw['conv_f_w'], lw['w_down'], lw['g_post_ffn'], st_f)


def _front_pad(state, rows):
    return jnp.pad(state, ((0, 0), (rows - state.shape[1], 0), (0, 0)))


def _prompt_layer(x, mod, lw, prev_kept):
    bsz, seq, d = x.shape
    d_a, d_b = lw['conv_a_w'].shape[1], lw['conv_b_w'].shape[1]
    d_ff = lw['w_gate'].shape[1]
    rows = PROMPT_TILE_ROWS if seq % PROMPT_TILE_ROWS == 0 else seq
    carry_a = _round_up(CONV_A_WIDTH - 1, SUBLANES)
    carry_b = _round_up(CONV_B_WIDTH - 1, SUBLANES)
    carry_f = _round_up(FFN_CONV_WIDTH - 1, SUBLANES)
    mod4 = mod.reshape(bsz, 6, 1, d)
    keep = min(DILATED_PATTERNS[-1][0], seq)
    assert keep < seq
    mixab, q, k, v, na, nb, kept_kt, kept_vt = _mixin_call(
        x, mod4, jnp.zeros((bsz, carry_a, d_a), F32), jnp.zeros((bsz, carry_b, d_b), F32), lw,
        rows=rows, shift=1, keep=keep, prev_kept=prev_kept)
    o = _attn_call(q, k, v)
    y, nf = _outffn_call(mixab, o, x, mod4, jnp.zeros((bsz, carry_f, d_ff), F32), lw, rows=rows, shift=1)
    return (y, (kept_kt, kept_vt), na[:, carry_a - (CONV_A_WIDTH - 1):], nb[:, carry_b - (CONV_B_WIDTH - 1):],
            nf[:, carry_f - (FFN_CONV_WIDTH - 1):])


def _time_major(state):
    bsz, k, c = state.shape
    return state.transpose(1, 0, 2).reshape(1, k * bsz, c)


def _batch_major(rows, bsz):
    _, n, c = rows.shape
    return rows.reshape(n // bsz, bsz, c).transpose(1, 0, 2)


def _sample_layer(x_tm, mod_tm, st_a, st_b, st_f, cache_kt, cache_vt, layer, lw, bsz):
    n = x_tm.shape[1]
    carry_a = _round_up((CONV_A_WIDTH - 1) * bsz, SUBLANES)
    carry_b = _round_up((CONV_B_WIDTH - 1) * bsz, SUBLANES)
    carry_f = _round_up((FFN_CONV_WIDTH - 1) * bsz, SUBLANES)
    mixab, q, k, v, na, nb = _mixin_call(
        x_tm, mod_tm, _front_pad(_time_major(st_a), carry_a), _front_pad(_time_major(st_b), carry_b), lw,
        rows=n, shift=bsz, keep=n)
    ungroup = lambda a: a.transpose(0, 2, 1, 3).reshape(1, n, -1)
    qb, kb, vb = (_batch_major(ungroup(a), bsz) for a in (q, k, v))
    n_q, n_heads = qb.shape[1], qb.shape[2] // HEAD_DIM
    per_head = lambda a: a.reshape(bsz, n_q, n_heads, HEAD_DIM).transpose(0, 2, 1, 3)
    o = _attn_sample_call(per_head(qb), per_head(kb), per_head(vb), cache_kt, cache_vt, layer)
    o_tm = o.transpose(2, 0, 1, 3).reshape(1, n, -1, LANES).transpose(0, 2, 1, 3)
    y, nf = _outffn_call(mixab, o_tm, x_tm, mod_tm, _front_pad(_time_major(st_f), carry_f), lw,
                         rows=n, shift=bsz)
    new_k = kb.reshape(bsz, kb.shape[1], n_heads, HEAD_DIM)
    new_v = vb.reshape(bsz, vb.shape[1], n_heads, HEAD_DIM)
    new_a = _batch_major(na[:, carry_a - (CONV_A_WIDTH - 1) * bsz:], bsz)
    new_b = _batch_major(nb[:, carry_b - (CONV_B_WIDTH - 1) * bsz:], bsz)
    new_f = _batch_major(nf[:, carry_f - (FFN_CONV_WIDTH - 1) * bsz:], bsz)
    return y, new_k, new_v, new_a, new_b, new_f


def kernel(x_prompt, x_sample, cache_k, cache_v, state_conv_a, state_conv_b, state_ffn_conv, c_prompt, c_sample, w_ada, b_ada, g_pre_mix, w_in, conv_a_w, conv_a_b, ln_a_g, ln_a_b, conv_b_w, g_out_a, g_out_b, g_out_c, w_o, g_post_mix, g_pre_ffn, w_gate, w_up, conv_f_w, w_down, g_post_ffn):
    depth = w_ada.shape[0]
    bsz_p, _, d = x_prompt.shape
    bsz_s, t_s, _ = x_sample.shape

    mod = _ada_call(jnp.concatenate([c_prompt, c_sample], axis=0), w_ada, b_ada)

    cache_kt = cache_k.transpose(0, 1, 3, 4, 2)
    cache_vt = cache_v.transpose(0, 1, 3, 4, 2)

    xp = x_prompt
    xs = x_sample.transpose(1, 0, 2).reshape(1, t_s * bsz_s, d)
    outs = [[] for _ in range(8)]
    kept = ()
    for l in range(depth):
        row = lambda a: a[l][None, :]
        lw = {
            'g_pre_mix': row(g_pre_mix), 'w_in': w_in[l].astype(BF16),
            'conv_a_w': conv_a_w[l], 'conv_a_b': row(conv_a_b), 'ln_a_g': row(ln_a_g), 'ln_a_b': row(ln_a_b),
            'conv_b_w': conv_b_w[l], 'g_out_a': row(g_out_a), 'g_out_b': row(g_out_b), 'g_out_c': row(g_out_c),
            'w_o': w_o[l].astype(BF16), 'g_post_mix': row(g_post_mix), 'g_pre_ffn': row(g_pre_ffn),
            'w_gate': w_gate[l].astype(BF16), 'w_up': w_up[l].astype(BF16), 'conv_f_w': conv_f_w[l],
            'w_down': w_down[l].astype(BF16), 'g_post_ffn': row(g_post_ffn),
        }
        mod_p = mod[l, :bsz_p].reshape(bsz_p, 6, d)
        mod_s = mod[l, bsz_p:].reshape(bsz_s, 6, d).transpose(1, 0, 2)
        mod_s = jnp.tile(mod_s[:, None], (1, t_s, 1, 1)).reshape(1, 6, t_s * bsz_s, d)

        xp, kept, ap, bp, fp = _prompt_layer(xp, mod_p, lw, kept)
        xs, ks_, vs_, as_, bs_, fs_ = _sample_layer(
            xs, mod_s, state_conv_a[l], state_conv_b[l], state_ffn_conv[l], cache_kt, cache_vt, l, lw, bsz_s)
        for lst, val in zip(outs, (ks_, vs_, ap, as_, bp, bs_, fp, fs_)):
            lst.append(val)
    ys = xs.reshape(t_s, bsz_s, d).transpose(1, 0, 2)
    kp, vp = (a.reshape(a.shape[:2] + (-1, HEAD_DIM, a.shape[3])).transpose(0, 1, 4, 2, 3) for a in kept)
    return (xp, ys, kp, vp) + tuple(jnp.stack(o) for o in outs)
```

```python
import functools

import jax
import jax.numpy as jnp
from jax import lax
from jax.experimental import pallas as pl
from jax.experimental.pallas import tpu as pltpu

F32 = jnp.float32
BF16 = jnp.bfloat16

HEAD_DIM = 64
LANES = 128
SUBLANES = 8
CONV_A_WIDTH = 31
CONV_B_WIDTH = 3
FFN_CONV_WIDTH = 3
DILATED_PATTERNS = ((128, 1), (512, 4), (2048, 16))
Q_BLK = 128
CHUNK = Q_BLK // DILATED_PATTERNS[1][1]
EPS = 1e-6
NEG = -1e30
LOG2_E = 1.4426950408889634
ATTN_UNROLL = 4
VMEM_LIMIT = 56 * 1024 * 1024
PROMPT_TILE_ROWS = 512
CONV_CHUNK_ROWS = 64


def _round_up(n, m):
    return (n + m - 1) // m * m


def _rms(x, g):
    return x * lax.rsqrt(jnp.mean(x * x, axis=-1, keepdims=True) + EPS) * g


def _sigmoid(x):
    return 1.0 / (1.0 + jnp.exp(-x))


def _dot(a, b):
    return jnp.dot(a, b, preferred_element_type=F32)


def _dot_nt(a, b):
    return lax.dot_general(a, b, (((1,), (1,)), ((), ())), preferred_element_type=F32)


def _const_spec(shape):
    zeros = (0,) * len(shape)
    return pl.BlockSpec(shape, lambda *_: zeros, pipeline_mode=pl.Buffered(1))


def _ada_kernel(c_ref, w_ref, b_ref, o_ref):
    c = c_ref[...]
    s = (c * _sigmoid(c)).astype(BF16)
    o_ref[0] = _dot(s, w_ref[0].astype(BF16)) + b_ref[0]


def _ada_call(c_all, w_ada, b_ada):
    depth, d, n = w_ada.shape
    rows = c_all.shape[0]
    tn = n // 4
    return pl.pallas_call(
        _ada_kernel,
        grid=(depth, n // tn),
        in_specs=[
            pl.BlockSpec((rows, d), lambda l, j: (0, 0)),
            pl.BlockSpec((1, d, tn), lambda l, j: (l, 0, j)),
            pl.BlockSpec((1, 1, tn), lambda l, j: (l, 0, j)),
        ],
        out_specs=pl.BlockSpec((1, rows, tn), lambda l, j: (l, 0, j)),
        out_shape=jax.ShapeDtypeStruct((depth, rows, n), F32),
        compiler_params=pltpu.CompilerParams(
            dimension_semantics=("arbitrary", "arbitrary"), vmem_limit_bytes=VMEM_LIMIT),
        name="ada_mod",
    )(c_all, w_ada, b_ada.reshape(depth, 1, n))


def _tap_groups(base, ntaps, shift):
    by_residue = {}
    for j in range(ntaps):
        off = base + j * shift
        by_residue.setdefault(off % SUBLANES, []).append((j, off - off % SUBLANES))
    return sorted(by_residue.items())


def _conv_chunk(buf_ref, w_ref, tap_groups, c0, chunk):
    acc = None
    for res, taps in tap_groups:
        ext = chunk + (SUBLANES if res else 0)
        part = None
        for j, off in taps:
            term = buf_ref[pl.ds(c0 + off, ext), :] * w_ref[j:j + 1, :]
            part = term if part is None else part + term
        if res:
            part = part[res:res + chunk]
        acc = part if acc is None else acc + part
    return acc


def _mixin_kernel(x_ref, mod_ref, gpre_ref, win_ref, caw_ref, cab_ref, lng_ref, lnb_ref, cbw_ref,
                  goa_ref, gob_ref, sta_ref, stb_ref, *rest,
                  rows, shift, d_a, d_b, d_c, carry_a, carry_b, n_tiles, first_kept_tile, n_prev):
    prev_refs, rest = (rest[:2], rest[2:]) if n_prev else ((), rest)
    mixab_ref, q_ref, k_ref, v_ref, na_ref, nb_ref = rest[:6]
    abuf, ubuf = rest[-2:]
    t = pl.program_id(1)

    @pl.when(t == 0)
    def _():
        abuf[0:carry_a, :] = sta_ref[0]
        ubuf[0:carry_b, :] = stb_ref[0]

    x = x_ref[0]
    h = (_rms(x, gpre_ref[...]) * (1.0 + mod_ref[0, 1]) + mod_ref[0, 0]).astype(BF16)

    o_b = 2 * d_a
    o_c = o_b + 3 * d_b
    za = _dot(h, win_ref[:, 0:2 * d_a])
    abuf[carry_a:carry_a + rows, :] = za[:, 0:d_a] * _sigmoid(za[:, d_a:2 * d_a])
    zb = _dot(h, win_ref[:, o_b:o_b + 3 * d_b])
    ubuf[carry_b:carry_b + rows, :] = zb[:, 2 * d_b:3 * d_b] * zb[:, 0:d_b]
    b_gate = zb[:, d_b:2 * d_b]

    q = _dot(h, win_ref[:, o_c:o_c + d_c])
    k = _dot(h, win_ref[:, o_c + d_c:o_c + 2 * d_c])
    v = _dot(h, win_ref[:, o_c + 2 * d_c:o_c + 3 * d_c])
    for ref, val in ((q_ref, q), (k_ref, k), (v_ref, v)):
        for grp in range(d_c // LANES):
            ref[0, grp] = val[:, grp * LANES:(grp + 1) * LANES]

    taps_a = _tap_groups(carry_a - (CONV_A_WIDTH - 1) * shift, CONV_A_WIDTH, shift)
    chunk = min(rows, CONV_CHUNK_ROWS)
    a = jnp.concatenate([_conv_chunk(abuf, caw_ref, taps_a, c0, chunk) for c0 in range(0, rows, chunk)], axis=0)
    a = a + cab_ref[...]
    mu = jnp.mean(a, axis=-1, keepdims=True)
    ac = a - mu
    var = jnp.mean(ac * ac, axis=-1, keepdims=True)
    a = ac * lax.rsqrt(var + EPS) * lng_ref[...] + lnb_ref[...]
    a = a * _sigmoid(a)
    mixab_ref[0, :, 0:d_a] = _rms(a, goa_ref[...]).astype(BF16)
    na_ref[0] = abuf[rows:rows + carry_a, :]

    taps_b = _tap_groups(carry_b - (CONV_B_WIDTH - 1) * shift, CONV_B_WIDTH, shift)
    u = _conv_chunk(ubuf, cbw_ref, taps_b, 0, rows)
    mixab_ref[0, :, d_a:d_a + d_b] = _rms(b_gate * u, gob_ref[...]).astype(BF16)
    nb_ref[0] = ubuf[rows:rows + carry_b, :]

    if first_kept_tile is not None:
        kept_kt_ref, kept_vt_ref = rest[6:8]

        @pl.when(t >= first_kept_tile)
        def _():
            for idx, (kept_ref, new) in enumerate(((kept_kt_ref, k), (kept_vt_ref, v))):
                for i in range(n_prev):
                    kept_ref[i, 0] = prev_refs[idx][i, 0]
                kept_ref[n_prev, 0] = new.T

    if n_tiles > 1:
        abuf[0:carry_a, :] = abuf[rows:rows + carry_a, :]
        ubuf[0:carry_b, :] = ubuf[rows:rows + carry_b, :]


def _mixin_call(x, mod, st_a, st_b, lw, *, rows, shift, keep, prev_kept=()):
    groups, n, d = x.shape
    rm = mod.shape[2]
    d_a = lw['conv_a_w'].shape[1]
    d_b = lw['conv_b_w'].shape[1]
    d_c = lw['g_out_c'].shape[1]
    p_in = lw['w_in'].shape[1]
    carry_a, carry_b = st_a.shape[1], st_b.shape[1]
    n_tiles = n // rows
    assert keep % rows == 0 and keep <= n
    first_kept_tile = (n - keep) // rows if keep < n else None
    n_prev = prev_kept[0].shape[0] if prev_kept else 0
    kern = functools.partial(_mixin_kernel, rows=rows, shift=shift, d_a=d_a, d_b=d_b, d_c=d_c,
                             carry_a=carry_a, carry_b=carry_b, n_tiles=n_tiles, first_kept_tile=first_kept_tile,
                             n_prev=n_prev)
    kept_specs, kept_shapes, prev_specs = [], [], []
    if first_kept_tile is not None:
        kept_map = lambda g, t: (0, g, 0, jnp.maximum(t - first_kept_tile, 0))
        kept_specs = [pl.BlockSpec((n_prev + 1, 1, d_c, rows), kept_map)] * 2
        kept_shapes = [jax.ShapeDtypeStruct((n_prev + 1, groups, d_c, keep), F32)] * 2
        prev_specs = [pl.BlockSpec((n_prev, 1, d_c, rows), kept_map)] * (2 if n_prev else 0)
    tile = lambda w: pl.BlockSpec((1, rows, w), lambda g, t: (g, t, 0))
    per_group = lambda r, w: pl.BlockSpec((1, r, w), lambda g, t: (g, 0, 0))
    grouped = pl.BlockSpec((1, d_c // LANES, rows, LANES), lambda g, t: (g, 0, t, 0))
    outs = pl.pallas_call(
        kern,
        grid=(groups, n_tiles),
        in_specs=[
            tile(d),
            pl.BlockSpec((1, 6, rm, d), lambda g, t: (g, 0, 0, 0)),
            _const_spec((1, d)),
            _const_spec((d, p_in)),
            _const_spec((CONV_A_WIDTH, d_a)),
            _const_spec((1, d_a)),
            _const_spec((1, d_a)),
            _const_spec((1, d_a)),
            _const_spec((CONV_B_WIDTH, d_b)),
            _const_spec((1, d_a)),
            _const_spec((1, d_b)),
            per_group(carry_a, d_a),
            per_group(carry_b, d_b),
        ] + prev_specs,
        out_specs=[
            tile(d_a + d_b), grouped, grouped, grouped,
            per_group(carry_a, d_a), per_group(carry_b, d_b),
        ] + kept_specs,
        out_shape=[
            jax.ShapeDtypeStruct((groups, n, d_a + d_b), BF16),
            jax.ShapeDtypeStruct((groups, d_c // LANES, n, LANES), F32),
            jax.ShapeDtypeStruct((groups, d_c // LANES, n, LANES), F32),
            jax.ShapeDtypeStruct((groups, d_c // LANES, n, LANES), F32),
            jax.ShapeDtypeStruct((groups, carry_a, d_a), F32),
            jax.ShapeDtypeStruct((groups, carry_b, d_b), F32),
        ] + kept_shapes,
        scratch_shapes=[
            pltpu.VMEM((carry_a + rows, d_a), F32),
            pltpu.VMEM((carry_b + rows, d_b), F32),
        ],
        compiler_params=pltpu.CompilerParams(
            dimension_semantics=("arbitrary", "arbitrary"), vmem_limit_bytes=VMEM_LIMIT),
        name="mix_in",
    )(x, mod, lw['g_pre_mix'], lw['w_in'], lw['conv_a_w'], lw['conv_a_b'], lw['ln_a_g'], lw['ln_a_b'],
      lw['conv_b_w'], lw['g_out_a'], lw['g_out_b'], st_a, st_b, *prev_kept)
    return outs


def _attn_kernel(q_ref, k_ref, v_ref, o_ref, qs0, qs1, kts, vs, acc, mrun, lrun, bias, pbuf, mbuf,
                 qmid, kmid, vmid, *, seq):
    n_blk = seq // Q_BLK
    lane = lax.broadcasted_iota(jnp.int32, (1, LANES), 1)
    head0 = lane < HEAD_DIM
    scale = HEAD_DIM ** -0.5 * LOG2_E
    mid = DILATED_PATTERNS[1][1]
    n_mid = seq // mid

    @pl.when(jnp.logical_and(pl.program_id(0) == 0, pl.program_id(1) == 0))
    def _():
        idx = lax.broadcasted_iota(jnp.int32, (Q_BLK, 2 * Q_BLK), 0)
        col = lax.broadcasted_iota(jnp.int32, (Q_BLK, 2 * Q_BLK), 1)
        for perm, row in enumerate((idx, (idx % CHUNK) * mid + idx // CHUNK)):
            band = jnp.logical_and(col >= row, col <= row + Q_BLK)
            bias[perm, 0] = jnp.where(band, 0.0, NEG)
            bias[perm, 1] = jnp.where(jnp.logical_and(band, col >= Q_BLK), 0.0, NEG)
        vs[0:Q_BLK, 0:LANES] = jnp.zeros((Q_BLK, LANES), BF16)
        vs[:, LANES:2 * LANES] = jnp.ones((seq + Q_BLK, LANES), BF16)
        kts[0] = jnp.zeros((LANES, Q_BLK), BF16)

    blk_per_mid = n_blk // mid
    staged = ((q_ref.at[0, 0], qmid), (k_ref.at[0, 0], kmid), (v_ref.at[0, 0], vmid))

    def stage(jb, carry):
        phase = jb // blk_per_mid
        src = pl.ds(phase + (jb - phase * blk_per_mid) * (Q_BLK * mid), Q_BLK, stride=mid)
        dst = pl.ds(pl.multiple_of(jb * Q_BLK, Q_BLK), Q_BLK)
        for ref, buf in staged:
            buf[dst, :] = ref[src, :]
        return carry

    lax.fori_loop(0, n_blk, stage, 0, unroll=ATTN_UNROLL)

    order = DILATED_PATTERNS[::-1]
    n_grp = n_blk // ATTN_UNROLL
    for step, (_, dil) in enumerate(order):
        blk_per_phase = n_blk // dil

        def mid_rows(j, dil=dil, blk_per_phase=blk_per_phase):
            if dil == 1:
                return [pl.ds(pl.multiple_of(c * n_mid + j * CHUNK, CHUNK), CHUNK) for c in range(mid)]
            if dil == mid:
                return [pl.ds(pl.multiple_of(j * Q_BLK, Q_BLK), Q_BLK)]
            ratio = dil // mid
            phase = j // blk_per_phase
            start = (phase % mid) * n_mid + phase // mid + (j - phase * blk_per_phase) * (Q_BLK * ratio)
            return [pl.ds(start, Q_BLK, stride=ratio)]

        def load_rows(ref, pieces):
            parts = [ref[p, :] for p in pieces]
            return parts[0] if len(parts) == 1 else jnp.concatenate(parts, axis=0)

        def store_rows(ref, pieces, val):
            n = Q_BLK // len(pieces)
            for c, p in enumerate(pieces):
                ref[p, :] = val[c * n:(c + 1) * n]

        def split(g, dil=dil, mid_rows=mid_rows, load_rows=load_rows):
            g = jnp.minimum(g, n_grp - 1)
            for u in range(ATTN_UNROLL):
                j = g * ATTN_UNROLL + u
                pieces = mid_rows(j)
                dst = pl.ds(pl.multiple_of(j * Q_BLK, Q_BLK), Q_BLK)
                dst_kv = pl.ds(pl.multiple_of((j + 1) * Q_BLK, Q_BLK), Q_BLK)
                qv = (load_rows(qmid, pieces) * scale).astype(BF16)
                qs0[dst, :] = qv[:, :HEAD_DIM]
                qs1[dst, :] = qv[:, HEAD_DIM:]
                if dil == 1:
                    k_blk, v_blk = k_ref[0, 0, dst, :], v_ref[0, 0, dst, :]
                else:
                    k_blk, v_blk = load_rows(kmid, pieces), load_rows(vmid, pieces)
                kts[j + 1] = k_blk.T.astype(BF16)
                vs[dst_kv, 0:LANES] = v_blk.astype(BF16)

        def scores(g, slot, dil=dil, blk_per_phase=blk_per_phase):
            g = jnp.minimum(g, n_grp - 1)
            for u in range(ATTN_UNROLL):
                j = g * ATTN_UNROLL + u
                qrows = pl.ds(pl.multiple_of(j * Q_BLK, Q_BLK), Q_BLK)
                kbt = jnp.concatenate([kts[j], kts[j + 1]], axis=1)
                bb = bias[1 if dil == 1 else 0, (j % blk_per_phase == 0).astype(jnp.int32)]
                ms = []
                for h, qs in enumerate((qs0, qs1)):
                    s = _dot(qs[qrows, :], kbt[h * HEAD_DIM:(h + 1) * HEAD_DIM, :]) + bb
                    m = jnp.max(s, axis=-1, keepdims=True)
                    pbuf[slot, u, h] = jnp.exp2(s - m).astype(BF16)
                    ms.append(m)
                mbuf[slot, u] = jnp.where(head0, ms[0], ms[1])

        def values(g, slot, step=step, mid_rows=mid_rows, load_rows=load_rows, store_rows=store_rows):
            for u in range(ATTN_UNROLL):
                j = g * ATTN_UNROLL + u
                vb = vs[pl.ds(pl.multiple_of(j * Q_BLK, Q_BLK), 2 * Q_BLK), :]
                pv0 = _dot(pbuf[slot, u, 0], vb)
                pv1 = _dot(pbuf[slot, u, 1], vb)
                a_new = jnp.where(head0, pv0[:, :LANES], pv1[:, :LANES])
                l_new = jnp.where(head0, pv0[:, LANES:], pv1[:, LANES:])
                m_new = mbuf[slot, u]
                dst = mid_rows(j)
                if step > 0:
                    m_old = load_rows(mrun, dst)
                    m_tot = jnp.maximum(m_old, m_new)
                    w_old = jnp.exp2(m_old - m_tot)
                    w_new = jnp.exp2(m_new - m_tot)
                    a_new = w_old * load_rows(acc, dst) + w_new * a_new
                    l_new = w_old * load_rows(lrun, dst) + w_new * l_new
                    m_new = m_tot
                if step < len(order) - 1:
                    store_rows(mrun, dst, m_new)
                    store_rows(lrun, dst, l_new)
                    store_rows(acc, dst, a_new)
                else:
                    out = [pl.ds(j * Q_BLK + c, CHUNK, stride=mid) for c in range(mid)]
                    store_rows(o_ref.at[0, 0], out, a_new / l_new)

        for g in range(3):
            split(jnp.int32(g))
        scores(jnp.int32(0), 0)

        def pair(t, carry, split=split, scores=scores, values=values):
            g = 2 * t
            scores(g + 1, 1)
            values(g, 0)
            scores(g + 2, 0)
            values(g + 1, 1)
            split(g + 3)
            split(g + 4)
            return carry

        lax.fori_loop(0, n_grp // 2, pair, 0)


def _attn_call(q, k, v):
    bsz, n_grp, seq, _ = q.shape
    assert seq % (Q_BLK * DILATED_PATTERNS[-1][1]) == 0 and seq % (2 * ATTN_UNROLL * Q_BLK) == 0
    assert len(DILATED_PATTERNS) == 3 and DILATED_PATTERNS[0][1] == 1
    assert DILATED_PATTERNS[2][1] % DILATED_PATTERNS[1][1] == 0
    spec = pl.BlockSpec((1, 1, seq, LANES), lambda b, h: (b, h, 0, 0))
    return pl.pallas_call(
        functools.partial(_attn_kernel, seq=seq),
        grid=(bsz, n_grp),
        in_specs=[spec, spec, spec],
        out_specs=spec,
        out_shape=jax.ShapeDtypeStruct(q.shape, F32),
        scratch_shapes=[
            pltpu.VMEM((seq, HEAD_DIM), BF16),
            pltpu.VMEM((seq, HEAD_DIM), BF16),
            pltpu.VMEM((seq // Q_BLK + 1, LANES, Q_BLK), BF16),
            pltpu.VMEM((seq + Q_BLK, 2 * LANES), BF16),
            pltpu.VMEM((seq, LANES), F32),
            pltpu.VMEM((seq, LANES), F32),
            pltpu.VMEM((seq, LANES), F32),
            pltpu.VMEM((2, 2, Q_BLK, 2 * Q_BLK), F32),
            pltpu.VMEM((2, ATTN_UNROLL, 2, Q_BLK, 2 * Q_BLK), BF16),
            pltpu.VMEM((2, ATTN_UNROLL, Q_BLK, LANES), F32),
            pltpu.VMEM((seq, LANES), F32),
            pltpu.VMEM((seq, LANES), F32),
            pltpu.VMEM((seq, LANES), F32),
        ],
        compiler_params=pltpu.CompilerParams(
            dimension_semantics=("arbitrary", "arbitrary"), vmem_limit_bytes=VMEM_LIMIT),
        name="attn_prompt",
    )(q, k, v)


def _multiplicity(dist):
    cnt = jnp.zeros(dist.shape, F32)
    for window, dil in DILATED_PATTERNS:
        hit = jnp.logical_and(dist >= 0, jnp.logical_and(dist <= window, dist % dil == 0))
        cnt = cnt + jnp.where(hit, 1.0, 0.0)
    return cnt


def _attn_sample_kernel(q_ref, kn_ref, vn_ref, kt_ref, vt_ref, o_ref, cnt_c, cnt_n, *, n_q, n_heads, w_buf):
    @pl.when(pl.program_id(0) == 0)
    def _():
        def table(n_keys, first_pos):
            q_pos = w_buf + lax.broadcasted_iota(jnp.int32, (n_q, n_keys), 0)
            k_pos = first_pos + lax.broadcasted_iota(jnp.int32, (n_q, n_keys), 1)
            return _multiplicity(q_pos - k_pos)

        cnt_c[...] = table(w_buf, 0)
        cnt_n[...] = table(n_q, w_buf)

    mult_c = cnt_c[...]
    mult_n = cnt_n[...]
    for h in range(n_heads):
        q = (q_ref[0, h] * HEAD_DIM ** -0.5).astype(BF16)
        s_c = jnp.where(mult_c > 0.0, _dot(q, kt_ref[0, 0, h].astype(BF16)), NEG)
        s_n = jnp.where(mult_n > 0.0, _dot_nt(q, kn_ref[0, h].astype(BF16)), NEG)
        m = jnp.maximum(jnp.max(s_c, axis=-1, keepdims=True), jnp.max(s_n, axis=-1, keepdims=True))
        p_c = mult_c * jnp.exp(s_c - m)
        p_n = mult_n * jnp.exp(s_n - m)
        l = jnp.sum(p_c, axis=-1, keepdims=True) + jnp.sum(p_n, axis=-1, keepdims=True)
        o = _dot_nt(p_c.astype(BF16), vt_ref[0, 0, h].astype(BF16)) + _dot(p_n.astype(BF16), vn_ref[0, h].astype(BF16))
        o_ref[0, h] = o / l


def _attn_sample_call(q, k_new, v_new, cache_kt, cache_vt, layer):
    bsz, n_heads, n_q, head_dim = q.shape
    w_buf = cache_kt.shape[4]
    new_spec = pl.BlockSpec((1, n_heads, n_q, head_dim), lambda b: (b, 0, 0, 0))
    cache_spec = pl.BlockSpec((1, 1, n_heads, head_dim, w_buf), lambda b: (layer, b, 0, 0, 0))
    return pl.pallas_call(
        functools.partial(_attn_sample_kernel, n_q=n_q, n_heads=n_heads, w_buf=w_buf),
        grid=(bsz,),
        in_specs=[new_spec, new_spec, new_spec, cache_spec, cache_spec],
        out_specs=new_spec,
        out_shape=jax.ShapeDtypeStruct(q.shape, F32),
        scratch_shapes=[pltpu.VMEM((n_q, w_buf), F32), pltpu.VMEM((n_q, n_q), F32)],
        compiler_params=pltpu.CompilerParams(
            dimension_semantics=("arbitrary",), vmem_limit_bytes=VMEM_LIMIT),
        name="attn_sample",
    )(q, k_new, v_new, cache_kt, cache_vt)


def _outffn_kernel(mixab_ref, o_ref, x_ref, mod_ref, goc_ref, wo_ref, gpm_ref, gpf_ref, wg_ref, wu_ref,
                   cfw_ref, wd_ref, gpo_ref, stf_ref,
                   y_ref, nf_ref, gbuf, *, rows, shift, carry_f, n_tiles):
    t = pl.program_id(1)

    @pl.when(t == 0)
    def _():
        gbuf[0:carry_f, :] = stf_ref[0]

    d_ab = mixab_ref.shape[2]
    o = jnp.concatenate([o_ref[0, grp] for grp in range(o_ref.shape[1])], axis=-1)
    oc = _rms(o, goc_ref[...]).astype(BF16)
    y = _dot(mixab_ref[0], wo_ref[0:d_ab, :]) + _dot(oc, wo_ref[d_ab:, :])
    x1 = x_ref[0] + mod_ref[0, 2] * _rms(y, gpm_ref[...])

    h = (_rms(x1, gpf_ref[...]) * (1.0 + mod_ref[0, 4]) + mod_ref[0, 3]).astype(BF16)
    gbuf[carry_f:carry_f + rows, :] = _dot(h, wg_ref[...])
    taps = _tap_groups(carry_f - (FFN_CONV_WIDTH - 1) * shift, FFN_CONV_WIDTH, shift)
    g = _conv_chunk(gbuf, cfw_ref, taps, 0, rows)
    f = (g * _sigmoid(g) * _dot(h, wu_ref[...])).astype(BF16)
    y2 = _dot(f, wd_ref[...])
    y_ref[0] = x1 + mod_ref[0, 5] * _rms(y2, gpo_ref[...])
    nf_ref[0] = gbuf[rows:rows + carry_f, :]
    if n_tiles > 1:
        gbuf[0:carry_f, :] = gbuf[rows:rows + carry_f, :]


def _outffn_call(mixab, o, x, mod, st_f, lw, *, rows, shift):
    groups, n, d = x.shape
    rm = mod.shape[2]
    d_ab = mixab.shape[2]
    n_og = o.shape[1]
    d_c = n_og * LANES
    d_ff = lw['w_gate'].shape[1]
    carry_f = st_f.shape[1]
    n_tiles = n // rows
    kern = functools.partial(_outffn_kernel, rows=rows, shift=shift, carry_f=carry_f, n_tiles=n_tiles)
    tile = lambda w: pl.BlockSpec((1, rows, w), lambda g, t: (g, t, 0))
    per_group = lambda r, w: pl.BlockSpec((1, r, w), lambda g, t: (g, 0, 0))
    return pl.pallas_call(
        kern,
        grid=(groups, n_tiles),
        in_specs=[
            tile(d_ab), pl.BlockSpec((1, n_og, rows, LANES), lambda g, t: (g, 0, t, 0)), tile(d),
            pl.BlockSpec((1, 6, rm, d), lambda g, t: (g, 0, 0, 0)),
            _const_spec((1, d_c)),
            _const_spec((d_ab + d_c, d)),
            _const_spec((1, d)),
            _const_spec((1, d)),
            _const_spec((d, d_ff)),
            _const_spec((d, d_ff)),
            _const_spec((FFN_CONV_WIDTH, d_ff)),
            _const_spec((d_ff, d)),
            _const_spec((1, d)),
            per_group(carry_f, d_ff),
        ],
        out_specs=[tile(d), per_group(carry_f, d_ff)],
        out_shape=[
            jax.ShapeDtypeStruct((groups, n, d), F32),
            jax.ShapeDtypeStruct((groups, carry_f, d_ff), F32),
        ],
        scratch_shapes=[pltpu.VMEM((carry_f + rows, d_ff), F32)],
        compiler_params=pltpu.CompilerParams(
            dimension_semantics=("arbitrary", "arbitrary"), vmem_limit_bytes=VMEM_LIMIT),
        name="out_ffn",
    )(mixab, o, x, mod, lw['g_out_c'], lw['w_o'], lw['g_post_mix'], lw['g_pre_ffn'], lw['w_gate'],
      lw['w_up'], lw['conv_f_w'], lw['w_down'], lw['g_post_ffn'], st_f)


def _front_pad(state, rows):
    return jnp.pad(state, ((0, 0), (rows - state.shape[1], 0), (0, 0)))


def _prompt_layer(x, mod, lw, prev_kept):
    bsz, seq, d = x.shape
    d_a, d_b = lw['conv_a_w'].shape[1], lw['conv_b_w'].shape[1]
    d_ff = lw['w_gate'].shape[1]
    rows = PROMPT_TILE_ROWS if seq % PROMPT_TILE_ROWS == 0 else seq
    carry_a = _round_up(CONV_A_WIDTH - 1, SUBLANES)
    carry_b = _round_up(CONV_B_WIDTH - 1, SUBLANES)
    carry_f = _round_up(FFN_CONV_WIDTH - 1, SUBLANES)
    mod4 = mod.reshape(bsz, 6, 1, d)
    keep = min(DILATED_PATTERNS[-1][0], seq)
    assert keep < seq
    mixab, q, k, v, na, nb, kept_kt, kept_vt = _mixin_call(
        x, mod4, jnp.zeros((bsz, carry_a, d_a), F32), jnp.zeros((bsz, carry_b, d_b), F32), lw,
        rows=rows, shift=1, keep=keep, prev_kept=prev_kept)
    o = _attn_call(q, k, v)
    y, nf = _outffn_call(mixab, o, x, mod4, jnp.zeros((bsz, carry_f, d_ff), F32), lw, rows=rows, shift=1)
    return (y, (kept_kt, kept_vt), na[:, carry_a - (CONV_A_WIDTH - 1):], nb[:, carry_b - (CONV_B_WIDTH - 1):],
            nf[:, carry_f - (FFN_CONV_WIDTH - 1):])


def _time_major(state):
    bsz, k, c = state.shape
    return state.transpose(1, 0, 2).reshape(1, k * bsz, c)


def _batch_major(rows, bsz):
    _, n, c = rows.shape
    return rows.reshape(n // bsz, bsz, c).transpose(1, 0, 2)


def _sample_layer(x_tm, mod_tm, st_a, st_b, st_f, cache_kt, cache_vt, layer, lw, bsz):
    n = x_tm.shape[1]
    carry_a = _round_up((CONV_A_WIDTH - 1) * bsz, SUBLANES)
    carry_b = _round_up((CONV_B_WIDTH - 1) * bsz, SUBLANES)
    carry_f = _round_up((FFN_CONV_WIDTH - 1) * bsz, SUBLANES)
    mixab, q, k, v, na, nb = _mixin_call(
        x_tm, mod_tm, _front_pad(_time_major(st_a), carry_a), _front_pad(_time_major(st_b), carry_b), lw,
        rows=n, shift=bsz, keep=n)
    ungroup = lambda a: a.transpose(0, 2, 1, 3).reshape(1, n, -1)
    qb, kb, vb = (_batch_major(ungroup(a), bsz) for a in (q, k, v))
    n_q, n_heads = qb.shape[1], qb.shape[2] // HEAD_DIM
    per_head = lambda a: a.reshape(bsz, n_q, n_heads, HEAD_DIM).transpose(0, 2, 1, 3)
    o = _attn_sample_call(per_head(qb), per_head(kb), per_head(vb), cache_kt, cache_vt, layer)
    o_tm = o.transpose(2, 0, 1, 3).reshape(1, n, -1, LANES).transpose(0, 2, 1, 3)
    y, nf = _outffn_call(mixab, o_tm, x_tm, mod_tm, _front_pad(_time_major(st_f), carry_f), lw,
                         rows=n, shift=bsz)
    new_k = kb.reshape(bsz, kb.shape[1], n_heads, HEAD_DIM)
    new_v = vb.reshape(bsz, vb.shape[1], n_heads, HEAD_DIM)
    new_a = _batch_major(na[:, carry_a - (CONV_A_WIDTH - 1) * bsz:], bsz)
    new_b = _batch_major(nb[:, carry_b - (CONV_B_WIDTH - 1) * bsz:], bsz)
    new_f = _batch_major(nf[:, carry_f - (FFN_CONV_WIDTH - 1) * bsz:], bsz)
    return y, new_k, new_v, new_a, new_b, new_f


def kernel(x_prompt, x_sample, cache_k, cache_v, state_conv_a, state_conv_b, state_ffn_conv, c_prompt, c_sample, w_ada, b_ada, g_pre_mix, w_in, conv_a_w, conv_a_b, ln_a_g, ln_a_b, conv_b_w, g_out_a, g_out_b, g_out_c, w_o, g_post_mix, g_pre_ffn, w_gate, w_up, conv_f_w, w_down, g_post_ffn):
    depth = w_ada.shape[0]
    bsz_p, _, d = x_prompt.shape
    bsz_s, t_s, _ = x_sample.shape

    mod = _ada_call(jnp.concatenate([c_prompt, c_sample], axis=0), w_ada, b_ada)

    cache_kt = cache_k.transpose(0, 1, 3, 4, 2)
    cache_vt = cache_v.transpose(0, 1, 3, 4, 2)

    xp = x_prompt
    xs = x_sample.transpose(1, 0, 2).reshape(1, t_s * bsz_s, d)
    outs = [[] for _ in range(8)]
    kept = ()
    for l in range(depth):
        row = lambda a: a[l][None, :]
        lw = {
            'g_pre_mix': row(g_pre_mix), 'w_in': w_in[l].astype(BF16),
            'conv_a_w': conv_a_w[l], 'conv_a_b': row(conv_a_b), 'ln_a_g': row(ln_a_g), 'ln_a_b': row(ln_a_b),
            'conv_b_w': conv_b_w[l], 'g_out_a': row(g_out_a), 'g_out_b': row(g_out_b), 'g_out_c': row(g_out_c),
            'w_o': w_o[l].astype(BF16), 'g_post_mix': row(g_post_mix), 'g_pre_ffn': row(g_pre_ffn),
            'w_gate': w_gate[l].astype(BF16), 'w_up': w_up[l].astype(BF16), 'conv_f_w': conv_f_w[l],
            'w_down': w_down[l].astype(BF16), 'g_post_ffn': row(g_post_ffn),
        }
        mod_p = mod[l, :bsz_p].reshape(bsz_p, 6, d)
        mod_s = mod[l, bsz_p:].reshape(bsz_s, 6, d).transpose(1, 0, 2)
        mod_s = jnp.tile(mod_s[:, None], (1, t_s, 1, 1)).reshape(1, 6, t_s * bsz_s, d)

        xp, kept, ap, bp, fp = _prompt_layer(xp, mod_p, lw, kept)
        xs, ks_, vs_, as_, bs_, fs_ = _sample_layer(
            xs, mod_s, state_conv_a[l], state_conv_b[l], state_ffn_conv[l], cache_kt, cache_vt, l, lw, bsz_s)
        for lst, val in zip(outs, (ks_, vs_, ap, as_, bp, bs_, fp, fs_)):
            lst.append(val)
    ys = xs.reshape(t_s, bsz_s, d).transpose(1, 0, 2)
    kp, vp = (a.reshape(a.shape[:2] + (-1, HEAD_DIM, a.shape[3])).transpose(0, 1, 4, 2, 3) for a in kept)
    return (xp, ys, kp, vp) + tuple(jnp.stack(o) for o in outs)
```

```python
import functools

import jax
import jax.numpy as jnp
from jax import lax
from jax.experimental import pallas as pl
from jax.experimental.pallas import tpu as pltpu

F32 = jnp.float32
BF16 = jnp.bfloat16

HEAD_DIM = 64
LANES = 128
SUBLANES = 8
CONV_A_WIDTH = 31
CONV_B_WIDTH = 3
FFN_CONV_WIDTH = 3
DILATED_PATTERNS = ((128, 1), (512, 4), (2048, 16))
Q_BLK = 128
CHUNK = Q_BLK // DILATED_PATTERNS[1][1]
EPS = 1e-6
NEG = -1e30
LOG2_E = 1.4426950408889634
ATTN_UNROLL = 4
VMEM_LIMIT = 56 * 1024 * 1024
PROMPT_TILE_ROWS = 512
CONV_CHUNK_ROWS = 64


def _round_up(n, m):
    return (n + m - 1) // m * m


def _rms(x, g):
    return x * lax.rsqrt(jnp.mean(x * x, axis=-1, keepdims=True) + EPS) * g


def _sigmoid(x):
    return 1.0 / (1.0 + jnp.exp(-x))


def _dot(a, b):
    return jnp.dot(a, b, preferred_element_type=F32)


def _dot_nt(a, b):
    return lax.dot_general(a, b, (((1,), (1,)), ((), ())), preferred_element_type=F32)


def _const_spec(shape):
    zeros = (0,) * len(shape)
    return pl.BlockSpec(shape, lambda *_: zeros, pipeline_mode=pl.Buffered(1))


def _ada_kernel(c_ref, w_ref, b_ref, o_ref):
    c = c_ref[...]
    s = (c * _sigmoid(c)).astype(BF16)
    o_ref[0] = _dot(s, w_ref[0].astype(BF16)) + b_ref[0]


def _ada_call(c_all, w_ada, b_ada):
    depth, d, n = w_ada.shape
    rows = c_all.shape[0]
    tn = n // 4
    return pl.pallas_call(
        _ada_kernel,
        grid=(depth, n // tn),
        in_specs=[
            pl.BlockSpec((rows, d), lambda l, j: (0, 0)),
            pl.BlockSpec((1, d, tn), lambda l, j: (l, 0, j)),
            pl.BlockSpec((1, 1, tn), lambda l, j: (l, 0, j)),
        ],
        out_specs=pl.BlockSpec((1, rows, tn), lambda l, j: (l, 0, j)),
        out_shape=jax.ShapeDtypeStruct((depth, rows, n), F32),
        compiler_params=pltpu.CompilerParams(
            dimension_semantics=("arbitrary", "arbitrary"), vmem_limit_bytes=VMEM_LIMIT),
        name="ada_mod",
    )(c_all, w_ada, b_ada.reshape(depth, 1, n))


def _tap_groups(base, ntaps, shift):
    by_residue = {}
    for j in range(ntaps):
        off = base + j * shift
        by_residue.setdefault(off % SUBLANES, []).append((j, off - off % SUBLANES))
    return sorted(by_residue.items())


def _conv_chunk(buf_ref, w_ref, tap_groups, c0, chunk):
    acc = None
    for res, taps in tap_groups:
        ext = chunk + (SUBLANES if res else 0)
        part = None
        for j, off in taps:
            term = buf_ref[pl.ds(c0 + off, ext), :] * w_ref[j:j + 1, :]
            part = term if part is None else part + term
        if res:
            part = part[res:res + chunk]
        acc = part if acc is None else acc + part
    return acc


def _mixin_kernel(x_ref, mod_ref, gpre_ref, win_ref, caw_ref, cab_ref, lng_ref, lnb_ref, cbw_ref,
                  goa_ref, gob_ref, sta_ref, stb_ref, *rest,
                  rows, shift, d_a, d_b, d_c, carry_a, carry_b, n_tiles, first_kept_tile, n_prev):
    prev_refs, rest = (rest[:2], rest[2:]) if n_prev else ((), rest)
    mixab_ref, q_ref, k_ref, v_ref, na_ref, nb_ref = rest[:6]
    abuf, ubuf = rest[-2:]
    t = pl.program_id(1)

    @pl.when(t == 0)
    def _():
        abuf[0:carry_a, :] = sta_ref[0]
        ubuf[0:carry_b, :] = stb_ref[0]

    x = x_ref[0]
    h = (_rms(x, gpre_ref[...]) * (1.0 + mod_ref[0, 1]) + mod_ref[0, 0]).astype(BF16)

    o_b = 2 * d_a
    o_c = o_b + 3 * d_b
    za = _dot(h, win_ref[:, 0:2 * d_a])
    abuf[carry_a:carry_a + rows, :] = za[:, 0:d_a] * _sigmoid(za[:, d_a:2 * d_a])
    zb = _dot(h, win_ref[:, o_b:o_b + 3 * d_b])
    ubuf[carry_b:carry_b + rows, :] = zb[:, 2 * d_b:3 * d_b] * zb[:, 0:d_b]
    b_gate = zb[:, d_b:2 * d_b]

    q = _dot(h, win_ref[:, o_c:o_c + d_c])
    k = _dot(h, win_ref[:, o_c + d_c:o_c + 2 * d_c])
    v = _dot(h, win_ref[:, o_c + 2 * d_c:o_c + 3 * d_c])
    for ref, val in ((q_ref, q), (k_ref, k), (v_ref, v)):
        for grp in range(d_c // LANES):
            ref[0, grp] = val[:, grp * LANES:(grp + 1) * LANES]

    taps_a = _tap_groups(carry_a - (CONV_A_WIDTH - 1) * shift, CONV_A_WIDTH, shift)
    chunk = min(rows, CONV_CHUNK_ROWS)
    a = jnp.concatenate([_conv_chunk(abuf, caw_ref, taps_a, c0, chunk) for c0 in range(0, rows, chunk)], axis=0)
    a = a + cab_ref[...]
    mu = jnp.mean(a, axis=-1, keepdims=True)
    ac = a - mu
    var = jnp.mean(ac * ac, axis=-1, keepdims=True)
    a = ac * lax.rsqrt(var + EPS) * lng_ref[...] + lnb_ref[...]
    a = a * _sigmoid(a)
    mixab_ref[0, :, 0:d_a] = _rms(a, goa_ref[...]).astype(BF16)
    na_ref[0] = abuf[rows:rows + carry_a, :]

    taps_b = _tap_groups(carry_b - (CONV_B_WIDTH - 1) * shift, CONV_B_WIDTH, shift)
    u = _conv_chunk(ubuf, cbw_ref, taps_b, 0, rows)
    mixab_ref[0, :, d_a:d_a + d_b] = _rms(b_gate * u, gob_ref[...]).astype(BF16)
    nb_ref[0] = ubuf[rows:rows + carry_b, :]

    if first_kept_tile is not None:
        kept_kt_ref, kept_vt_ref = rest[6:8]

        @pl.when(t >= first_kept_tile)
        def _():
            for idx, (kept_ref, new) in enumerate(((kept_kt_ref, k), (kept_vt_ref, v))):
                for i in range(n_prev):
                    kept_ref[i, 0] = prev_refs[idx][i, 0]
                kept_ref[n_prev, 0] = new.T

    if n_tiles > 1:
        abuf[0:carry_a, :] = abuf[rows:rows + carry_a, :]
        ubuf[0:carry_b, :] = ubuf[rows:rows + carry_b, :]


def _mixin_call(x, mod, st_a, st_b, lw, *, rows, shift, keep, prev_kept=()):
    groups, n, d = x.shape
    rm = mod.shape[2]
    d_a = lw['conv_a_w'].shape[1]
    d_b = lw['conv_b_w'].shape[1]
    d_c = lw['g_out_c'].shape[1]
    p_in = lw['w_in'].shape[1]
    carry_a, carry_b = st_a.shape[1], st_b.shape[1]
    n_tiles = n // rows
    assert keep % rows == 0 and keep <= n
    first_kept_tile = (n - keep) // rows if keep < n else None
    n_prev = prev_kept[0].shape[0] if prev_kept else 0
    kern = functools.partial(_mixin_kernel, rows=rows, shift=shift, d_a=d_a, d_b=d_b, d_c=d_c,
                             carry_a=carry_a, carry_b=carry_b, n_tiles=n_tiles, first_kept_tile=first_kept_tile,
                             n_prev=n_prev)
    kept_specs, kept_shapes, prev_specs = [], [], []
    if first_kept_tile is not None:
        kept_map = lambda g, t: (0, g, 0, jnp.maximum(t - first_kept_tile, 0))
        kept_specs = [pl.BlockSpec((n_prev + 1, 1, d_c, rows), kept_map)] * 2
        kept_shapes = [jax.ShapeDtypeStruct((n_prev + 1, groups, d_c, keep), F32)] * 2
        prev_specs = [pl.BlockSpec((n_prev, 1, d_c, rows), kept_map)] * (2 if n_prev else 0)
    tile = lambda w: pl.BlockSpec((1, rows, w), lambda g, t: (g, t, 0))
    per_group = lambda r, w: pl.BlockSpec((1, r, w), lambda g, t: (g, 0, 0))
    grouped = pl.BlockSpec((1, d_c // LANES, rows, LANES), lambda g, t: (g, 0, t, 0))
    outs = pl.pallas_call(
        kern,
        grid=(groups, n_tiles),
        in_specs=[
            tile(d),
            pl.BlockSpec((1, 6, rm, d), lambda g, t: (g, 0, 0, 0)),
            _const_spec((1, d)),
            _const_spec((d, p_in)),
            _const_spec((CONV_A_WIDTH, d_a)),
            _const_spec((1, d_a)),
            _const_spec((1, d_a)),
            _const_spec((1, d_a)),
            _const_spec((CONV_B_WIDTH, d_b)),
            _const_spec((1, d_a)),
            _const_spec((1, d_b)),
            per_group(carry_a, d_a),
            per_group(carry_b, d_b),
        ] + prev_specs,
        out_specs=[
            tile(d_a + d_b), grouped, grouped, grouped,
            per_group(carry_a, d_a), per_group(carry_b, d_b),
        ] + kept_specs,
        out_shape=[
            jax.ShapeDtypeStruct((groups, n, d_a + d_b), BF16),
            jax.ShapeDtypeStruct((groups, d_c // LANES, n, LANES), F32),
            jax.ShapeDtypeStruct((groups, d_c // LANES, n, LANES), F32),
            jax.ShapeDtypeStruct((groups, d_c // LANES, n, LANES), F32),
            jax.ShapeDtypeStruct((groups, carry_a, d_a), F32),
            jax.ShapeDtypeStruct((groups, carry_b, d_b), F32),
        ] + kept_shapes,
        scratch_shapes=[
            pltpu.VMEM((carry_a + rows, d_a), F32),
            pltpu.VMEM((carry_b + rows, d_b), F32),
        ],
        compiler_params=pltpu.CompilerParams(
            dimension_semantics=("arbitrary", "arbitrary"), vmem_limit_bytes=VMEM_LIMIT),
        name="mix_in",
    )(x, mod, lw['g_pre_mix'], lw['w_in'], lw['conv_a_w'], lw['conv_a_b'], lw['ln_a_g'], lw['ln_a_b'],
      lw['conv_b_w'], lw['g_out_a'], lw['g_out_b'], st_a, st_b, *prev_kept)
    return outs


def _attn_kernel(q_ref, k_ref, v_ref, o_ref, qs0, qs1, kts, vs, acc, mrun, lrun, bias, pbuf, mbuf,
                 qmid, kmid, vmid, *, seq):
    n_blk = seq // Q_BLK
    lane = lax.broadcasted_iota(jnp.int32, (1, LANES), 1)
    head0 = lane < HEAD_DIM
    scale = HEAD_DIM ** -0.5 * LOG2_E
    mid = DILATED_PATTERNS[1][1]
    n_mid = seq // mid

    @pl.when(jnp.logical_and(pl.program_id(0) == 0, pl.program_id(1) == 0))
    def _():
        idx = lax.broadcasted_iota(jnp.int32, (Q_BLK, 2 * Q_BLK), 0)
        col = lax.broadcasted_iota(jnp.int32, (Q_BLK, 2 * Q_BLK), 1)
        for perm, row in enumerate((idx, (idx % CHUNK) * mid + idx // CHUNK)):
            band = jnp.logical_and(col >= row, col <= row + Q_BLK)
            bias[perm, 0] = jnp.where(band, 0.0, NEG)
            bias[perm, 1] = jnp.where(jnp.logical_and(band, col >= Q_BLK), 0.0, NEG)
        vs[0:Q_BLK, 0:LANES] = jnp.zeros((Q_BLK, LANES), BF16)
        vs[:, LANES:2 * LANES] = jnp.ones((seq + Q_BLK, LANES), BF16)
        kts[0] = jnp.zeros((LANES, Q_BLK), BF16)

    blk_per_mid = n_blk // mid
    staged = ((q_ref.at[0, 0], qmid), (k_ref.at[0, 0], kmid), (v_ref.at[0, 0], vmid))

    def stage(jb, carry):
        phase = jb // blk_per_mid
        src = pl.ds(phase + (jb - phase * blk_per_mid) * (Q_BLK * mid), Q_BLK, stride=mid)
        dst = pl.ds(pl.multiple_of(jb * Q_BLK, Q_BLK), Q_BLK)
        for ref, buf in staged:
            buf[dst, :] = ref[src, :]
        return carry

    lax.fori_loop(0, n_blk, stage, 0, unroll=ATTN_UNROLL)

    order = DILATED_PATTERNS[::-1]
    n_grp = n_blk // ATTN_UNROLL
    for step, (_, dil) in enumerate(order):
        blk_per_phase = n_blk // dil

        def mid_rows(j, dil=dil, blk_per_phase=blk_per_phase):
            if dil == 1:
                return [pl.ds(pl.multiple_of(c * n_mid + j * CHUNK, CHUNK), CHUNK) for c in range(mid)]
            if dil == mid:
                return [pl.ds(pl.multiple_of(j * Q_BLK, Q_BLK), Q_BLK)]
            ratio = dil // mid
            phase = j // blk_per_phase
            start = (phase % mid) * n_mid + phase // mid + (j - phase * blk_per_phase) * (Q_BLK * ratio)
            return [pl.ds(start, Q_BLK, stride=ratio)]

        def load_rows(ref, pieces):
            parts = [ref[p, :] for p in pieces]
            return parts[0] if len(parts) == 1 else jnp.concatenate(parts, axis=0)

        def store_rows(ref, pieces, val):
            n = Q_BLK // len(pieces)
            for c, p in enumerate(pieces):
                ref[p, :] = val[c * n:(c + 1) * n]

        def split(g, dil=dil, mid_rows=mid_rows, load_rows=load_rows):
            g = jnp.minimum(g, n_grp - 1)
            for u in range(ATTN_UNROLL):
                j = g * ATTN_UNROLL + u
                pieces = mid_rows(j)
                dst = pl.ds(pl.multiple_of(j * Q_BLK, Q_BLK), Q_BLK)
                dst_kv = pl.ds(pl.multiple_of((j + 1) * Q_BLK, Q_BLK), Q_BLK)
                qv = (load_rows(qmid, pieces) * scale).astype(BF16)
                qs0[dst, :] = qv[:, :HEAD_DIM]
                qs1[dst, :] = qv[:, HEAD_DIM:]
                if dil == 1:
                    k_blk, v_blk = k_ref[0, 0, dst, :], v_ref[0, 0, dst, :]
                else:
                    k_blk, v_blk = load_rows(kmid, pieces), load_rows(vmid, pieces)
                kts[j + 1] = k_blk.T.astype(BF16)
                vs[dst_kv, 0:LANES] = v_blk.astype(BF16)

        def scores(g, slot, dil=dil, blk_per_phase=blk_per_phase):
            g = jnp.minimum(g, n_grp - 1)
            for u in range(ATTN_UNROLL):
                j = g * ATTN_UNROLL + u
                qrows = pl.ds(pl.multiple_of(j * Q_BLK, Q_BLK), Q_BLK)
                kbt = jnp.concatenate([kts[j], kts[j + 1]], axis=1)
                bb = bias[1 if dil == 1 else 0, (j % blk_per_phase == 0).astype(jnp.int32)]
                ms = []
                for h, qs in enumerate((qs0, qs1)):
                    s = _dot(qs[qrows, :], kbt[h * HEAD_DIM:(h + 1) * HEAD_DIM, :]) + bb
                    m = jnp.max(s, axis=-1, keepdims=True)
                    pbuf[slot, u, h] = jnp.exp2(s - m).astype(BF16)
                    ms.append(m)
                mbuf[slot, u] = jnp.where(head0, ms[0], ms[1])

        def values(g, slot, step=step, mid_rows=mid_rows, load_rows=load_rows, store_rows=store_rows):
            for u in range(ATTN_UNROLL):
                j = g * ATTN_UNROLL + u
                vb = vs[pl.ds(pl.multiple_of(j * Q_BLK, Q_BLK), 2 * Q_BLK), :]
                pv0 = _dot(pbuf[slot, u, 0], vb)
                pv1 = _dot(pbuf[slot, u, 1], vb)
                a_new = jnp.where(head0, pv0[:, :LANES], pv1[:, :LANES])
                l_new = jnp.where(head0, pv0[:, LANES:], pv1[:, LANES:])
                m_new = mbuf[slot, u]
                dst = mid_rows(j)
                if step > 0:
                    m_old = load_rows(mrun, dst)
                    m_tot = jnp.maximum(m_old, m_new)
                    w_old = jnp.exp2(m_old - m_tot)
                    w_new = jnp.exp2(m_new - m_tot)
                    a_new = w_old * load_rows(acc, dst) + w_new * a_new
                    l_new = w_old * load_rows(lrun, dst) + w_new * l_new
                    m_new = m_tot
                if step < len(order) - 1:
                    store_rows(mrun, dst, m_new)
                    store_rows(lrun, dst, l_new)
                    store_rows(acc, dst, a_new)
                else:
                    out = [pl.ds(j * Q_BLK + c, CHUNK, stride=mid) for c in range(mid)]
                    store_rows(o_ref.at[0, 0], out, a_new / l_new)

        for g in range(3):
            split(jnp.int32(g))
        scores(jnp.int32(0), 0)

        def pair(t, carry, split=split, scores=scores, values=values):
            g = 2 * t
            scores(g + 1, 1)
            values(g, 0)
            scores(g + 2, 0)
            values(g + 1, 1)
            split(g + 3)
            split(g + 4)
            return carry

        lax.fori_loop(0, n_grp // 2 - 1, pair, 0)
        scores(jnp.int32(n_grp - 1), 1)
        values(jnp.int32(n_grp - 2), 0)
        values(jnp.int32(n_grp - 1), 1)


def _attn_call(q, k, v):
    bsz, n_grp, seq, _ = q.shape
    assert seq % (Q_BLK * DILATED_PATTERNS[-1][1]) == 0 and seq % (2 * ATTN_UNROLL * Q_BLK) == 0
    assert len(DILATED_PATTERNS) == 3 and DILATED_PATTERNS[0][1] == 1
    assert DILATED_PATTERNS[2][1] % DILATED_PATTERNS[1][1] == 0
    spec = pl.BlockSpec((1, 1, seq, LANES), lambda b, h: (b, h, 0, 0))
    return pl.pallas_call(
        functools.partial(_attn_kernel, seq=seq),
        grid=(bsz, n_grp),
        in_specs=[spec, spec, spec],
        out_specs=spec,
        out_shape=jax.ShapeDtypeStruct(q.shape, F32),
        scratch_shapes=[
            pltpu.VMEM((seq, HEAD_DIM), BF16),
            pltpu.VMEM((seq, HEAD_DIM), BF16),
            pltpu.VMEM((seq // Q_BLK + 1, LANES, Q_BLK), BF16),
            pltpu.VMEM((seq + Q_BLK, 2 * LANES), BF16),
            pltpu.VMEM((seq, LANES), F32),
            pltpu.VMEM((seq, LANES), F32),
            pltpu.VMEM((seq, LANES), F32),
            pltpu.VMEM((2, 2, Q_BLK, 2 * Q_BLK), F32),
            pltpu.VMEM((2, ATTN_UNROLL, 2, Q_BLK, 2 * Q_BLK), BF16),
            pltpu.VMEM((2, ATTN_UNROLL, Q_BLK, LANES), F32),
            pltpu.VMEM((seq, LANES), F32),
            pltpu.VMEM((seq, LANES), F32),
            pltpu.VMEM((seq, LANES), F32),
        ],
        compiler_params=pltpu.CompilerParams(
            dimension_semantics=("arbitrary", "arbitrary"), vmem_limit_bytes=VMEM_LIMIT),
        name="attn_prompt",
    )(q, k, v)


def _multiplicity(dist):
    cnt = jnp.zeros(dist.shape, F32)
    for window, dil in DILATED_PATTERNS:
        hit = jnp.logical_and(dist >= 0, jnp.logical_and(dist <= window, dist % dil == 0))
        cnt = cnt + jnp.where(hit, 1.0, 0.0)
    return cnt


def _attn_sample_kernel(q_ref, kn_ref, vn_ref, kt_ref, vt_ref, o_ref, cnt_c, cnt_n, *, n_q, n_heads, w_buf):
    @pl.when(pl.program_id(0) == 0)
    def _():
        def table(n_keys, first_pos):
            q_pos = w_buf + lax.broadcasted_iota(jnp.int32, (n_q, n_keys), 0)
            k_pos = first_pos + lax.broadcasted_iota(jnp.int32, (n_q, n_keys), 1)
            return _multiplicity(q_pos - k_pos)

        cnt_c[...] = table(w_buf, 0)
        cnt_n[...] = table(n_q, w_buf)

    mult_c = cnt_c[...]
    mult_n = cnt_n[...]
    for h in range(n_heads):
        q = (q_ref[0, h] * HEAD_DIM ** -0.5).astype(BF16)
        s_c = jnp.where(mult_c > 0.0, _dot(q, kt_ref[0, 0, h].astype(BF16)), NEG)
        s_n = jnp.where(mult_n > 0.0, _dot_nt(q, kn_ref[0, h].astype(BF16)), NEG)
        m = jnp.maximum(jnp.max(s_c, axis=-1, keepdims=True), jnp.max(s_n, axis=-1, keepdims=True))
        p_c = mult_c * jnp.exp(s_c - m)
        p_n = mult_n * jnp.exp(s_n - m)
        l = jnp.sum(p_c, axis=-1, keepdims=True) + jnp.sum(p_n, axis=-1, keepdims=True)
        o = _dot_nt(p_c.astype(BF16), vt_ref[0, 0, h].astype(BF16)) + _dot(p_n.astype(BF16), vn_ref[0, h].astype(BF16))
        o_ref[0, h] = o / l


def _attn_sample_call(q, k_new, v_new, cache_kt, cache_vt, layer):
    bsz, n_heads, n_q, head_dim = q.shape
    w_buf = cache_kt.shape[4]
    new_spec = pl.BlockSpec((1, n_heads, n_q, head_dim), lambda b: (b, 0, 0, 0))
    cache_spec = pl.BlockSpec((1, 1, n_heads, head_dim, w_buf), lambda b: (layer, b, 0, 0, 0))
    return pl.pallas_call(
        functools.partial(_attn_sample_kernel, n_q=n_q, n_heads=n_heads, w_buf=w_buf),
        grid=(bsz,),
        in_specs=[new_spec, new_spec, new_spec, cache_spec, cache_spec],
        out_specs=new_spec,
        out_shape=jax.ShapeDtypeStruct(q.shape, F32),
        scratch_shapes=[pltpu.VMEM((n_q, w_buf), F32), pltpu.VMEM((n_q, n_q), F32)],
        compiler_params=pltpu.CompilerParams(
            dimension_semantics=("arbitrary",), vmem_limit_bytes=VMEM_LIMIT),
        name="attn_sample",
    )(q, k_new, v_new, cache_kt, cache_vt)


def _outffn_kernel(mixab_ref, o_ref, x_ref, mod_ref, goc_ref, wo_ref, gpm_ref, gpf_ref, wg_ref, wu_ref,
                   cfw_ref, wd_ref, gpo_ref, stf_ref,
                   y_ref, nf_ref, gbuf, *, rows, shift, carry_f, n_tiles):
    t = pl.program_id(1)

    @pl.when(t == 0)
    def _():
        gbuf[0:carry_f, :] = stf_ref[0]

    d_ab = mixab_ref.shape[2]
    o = jnp.concatenate([o_ref[0, grp] for grp in range(o_ref.shape[1])], axis=-1)
    oc = _rms(o, goc_ref[...]).astype(BF16)
    y = _dot(mixab_ref[0], wo_ref[0:d_ab, :]) + _dot(oc, wo_ref[d_ab:, :])
    x1 = x_ref[0] + mod_ref[0, 2] * _rms(y, gpm_ref[...])

    h = (_rms(x1, gpf_ref[...]) * (1.0 + mod_ref[0, 4]) + mod_ref[0, 3]).astype(BF16)
    gbuf[carry_f:carry_f + rows, :] = _dot(h, wg_ref[...])
    taps = _tap_groups(carry_f - (FFN_CONV_WIDTH - 1) * shift, FFN_CONV_WIDTH, shift)
    g = _conv_chunk(gbuf, cfw_ref, taps, 0, rows)
    f = (g * _sigmoid(g) * _dot(h, wu_ref[...])).astype(BF16)
    y2 = _dot(f, wd_ref[...])
    y_ref[0] = x1 + mod_ref[0, 5] * _rms(y2, gpo_ref[...])
    nf_ref[0] = gbuf[rows:rows + carry_f, :]
    if n_tiles > 1:
        gbuf[0:carry_f, :] = gbuf[rows:rows + carry_f, :]


def _outffn_call(mixab, o, x, mod, st_f, lw, *, rows, shift):
    groups, n, d = x.shape
    rm = mod.shape[2]
    d_ab = mixab.shape[2]
    n_og = o.shape[1]
    d_c = n_og * LANES
    d_ff = lw['w_gate'].shape[1]
    carry_f = st_f.shape[1]
    n_tiles = n // rows
    kern = functools.partial(_outffn_kernel, rows=rows, shift=shift, carry_f=carry_f, n_tiles=n_tiles)
    tile = lambda w: pl.BlockSpec((1, rows, w), lambda g, t: (g, t, 0))
    per_group = lambda r, w: pl.BlockSpec((1, r, w), lambda g, t: (g, 0, 0))
    return pl.pallas_call(
        kern,
        grid=(groups, n_tiles),
        in_specs=[
            tile(d_ab), pl.BlockSpec((1, n_og, rows, LANES), lambda g, t: (g, 0, t, 0)), tile(d),
            pl.BlockSpec((1, 6, rm, d), lambda g, t: (g, 0, 0, 0)),
            _const_spec((1, d_c)),
            _const_spec((d_ab + d_c, d)),
            _const_spec((1, d)),
            _const_spec((1, d)),
            _const_spec((d, d_ff)),
            _const_spec((d, d_ff)),
            _const_spec((FFN_CONV_WIDTH, d_ff)),
            _const_spec((d_ff, d)),
            _const_spec((1, d)),
            per_group(carry_f, d_ff),
        ],
        out_specs=[tile(d), per_group(carry_f, d_ff)],
        out_shape=[
            jax.ShapeDtypeStruct((groups, n, d), F32),
            jax.ShapeDtypeStruct((groups, carry_f, d_ff), F32),
        ],
        scratch_shapes=[pltpu.VMEM((carry_f + rows, d_ff), F32)],
        compiler_params=pltpu.CompilerParams(
            dimension_semantics=("arbitrary", "arbitrary"), vmem_limit_bytes=VMEM_LIMIT),
        name="out_ffn",
    )(mixab, o, x, mod, lw['g_out_c'], lw['w_o'], lw['g_post_mix'], lw['g_pre_ffn'], lw['w_gate'],
      lw['w_up'], lw['conv_f_w'], lw['w_down'], lw['g_post_ffn'], st_f)


def _front_pad(state, rows):
    return jnp.pad(state, ((0, 0), (rows - state.shape[1], 0), (0, 0)))


def _prompt_layer(x, mod, lw, prev_kept):
    bsz, seq, d = x.shape
    d_a, d_b = lw['conv_a_w'].shape[1], lw['conv_b_w'].shape[1]
    d_ff = lw['w_gate'].shape[1]
    rows = PROMPT_TILE_ROWS if seq % PROMPT_TILE_ROWS == 0 else seq
    carry_a = _round_up(CONV_A_WIDTH - 1, SUBLANES)
    carry_b = _round_up(CONV_B_WIDTH - 1, SUBLANES)
    carry_f = _round_up(FFN_CONV_WIDTH - 1, SUBLANES)
    mod4 = mod.reshape(bsz, 6, 1, d)
    keep = min(DILATED_PATTERNS[-1][0], seq)
    assert keep < seq
    mixab, q, k, v, na, nb, kept_kt, kept_vt = _mixin_call(
        x, mod4, jnp.zeros((bsz, carry_a, d_a), F32), jnp.zeros((bsz, carry_b, d_b), F32), lw,
        rows=rows, shift=1, keep=keep, prev_kept=prev_kept)
    o = _attn_call(q, k, v)
    y, nf = _outffn_call(mixab, o, x, mod4, jnp.zeros((bsz, carry_f, d_ff), F32), lw, rows=rows, shift=1)
    return (y, (kept_kt, kept_vt), na[:, carry_a - (CONV_A_WIDTH - 1):], nb[:, carry_b - (CONV_B_WIDTH - 1):],
            nf[:, carry_f - (FFN_CONV_WIDTH - 1):])


def _time_major(state):
    bsz, k, c = state.shape
    return state.transpose(1, 0, 2).reshape(1, k * bsz, c)


def _batch_major(rows, bsz):
    _, n, c = rows.shape
    return rows.reshape(n // bsz, bsz, c).transpose(1, 0, 2)


def _sample_layer(x_tm, mod_tm, st_a, st_b, st_f, cache_kt, cache_vt, layer, lw, bsz):
    n = x_tm.shape[1]
    carry_a = _round_up((CONV_A_WIDTH - 1) * bsz, SUBLANES)
    carry_b = _round_up((CONV_B_WIDTH - 1) * bsz, SUBLANES)
    carry_f = _round_up((FFN_CONV_WIDTH - 1) * bsz, SUBLANES)
    mixab, q, k, v, na, nb = _mixin_call(
        x_tm, mod_tm, _front_pad(_time_major(st_a), carry_a), _front_pad(_time_major(st_b), carry_b), lw,
        rows=n, shift=bsz, keep=n)
    ungroup = lambda a: a.transpose(0, 2, 1, 3).reshape(1, n, -1)
    qb, kb, vb = (_batch_major(ungroup(a), bsz) for a in (q, k, v))
    n_q, n_heads = qb.shape[1], qb.shape[2] // HEAD_DIM
    per_head = lambda a: a.reshape(bsz, n_q, n_heads, HEAD_DIM).transpose(0, 2, 1, 3)
    o = _attn_sample_call(per_head(qb), per_head(kb), per_head(vb), cache_kt, cache_vt, layer)
    o_tm = o.transpose(2, 0, 1, 3).reshape(1, n, -1, LANES).transpose(0, 2, 1, 3)
    y, nf = _outffn_call(mixab, o_tm, x_tm, mod_tm, _front_pad(_time_major(st_f), carry_f), lw,
                         rows=n, shift=bsz)
    new_k = kb.reshape(bsz, kb.shape[1], n_heads, HEAD_DIM)
    new_v = vb.reshape(bsz, vb.shape[1], n_heads, HEAD_DIM)
    new_a = _batch_major(na[:, carry_a - (CONV_A_WIDTH - 1) * bsz:], bsz)
    new_b = _batch_major(nb[:, carry_b - (CONV_B_WIDTH - 1) * bsz:], bsz)
    new_f = _batch_major(nf[:, carry_f - (FFN_CONV_WIDTH - 1) * bsz:], bsz)
    return y, new_k, new_v, new_a, new_b, new_f


def kernel(x_prompt, x_sample, cache_k, cache_v, state_conv_a, state_conv_b, state_ffn_conv, c_prompt, c_sample, w_ada, b_ada, g_pre_mix, w_in, conv_a_w, conv_a_b, ln_a_g, ln_a_b, conv_b_w, g_out_a, g_out_b, g_out_c, w_o, g_post_mix, g_pre_ffn, w_gate, w_up, conv_f_w, w_down, g_post_ffn):
    depth = w_ada.shape[0]
    bsz_p, _, d = x_prompt.shape
    bsz_s, t_s, _ = x_sample.shape

    mod = _ada_call(jnp.concatenate([c_prompt, c_sample], axis=0), w_ada, b_ada)

    cache_kt = cache_k.transpose(0, 1, 3, 4, 2)
    cache_vt = cache_v.transpose(0, 1, 3, 4, 2)

    xp = x_prompt
    xs = x_sample.transpose(1, 0, 2).reshape(1, t_s * bsz_s, d)
    outs = [[] for _ in range(8)]
    kept = ()
    for l in range(depth):
        row = lambda a: a[l][None, :]
        lw = {
            'g_pre_mix': row(g_pre_mix), 'w_in': w_in[l].astype(BF16),
            'conv_a_w': conv_a_w[l], 'conv_a_b': row(conv_a_b), 'ln_a_g': row(ln_a_g), 'ln_a_b': row(ln_a_b),
            'conv_b_w': conv_b_w[l], 'g_out_a': row(g_out_a), 'g_out_b': row(g_out_b), 'g_out_c': row(g_out_c),
            'w_o': w_o[l].astype(BF16), 'g_post_mix': row(g_post_mix), 'g_pre_ffn': row(g_pre_ffn),
            'w_gate': w_gate[l].astype(BF16), 'w_up': w_up[l].astype(BF16), 'conv_f_w': conv_f_w[l],
            'w_down': w_down[l].astype(BF16), 'g_post_ffn': row(g_post_ffn),
        }
        mod_p = mod[l, :bsz_p].reshape(bsz_p, 6, d)
        mod_s = mod[l, bsz_p:].reshape(bsz_s, 6, d).transpose(1, 0, 2)
        mod_s = jnp.tile(mod_s[:, None], (1, t_s, 1, 1)).reshape(1, 6, t_s * bsz_s, d)

        xp, kept, ap, bp, fp = _prompt_layer(xp, mod_p, lw, kept)
        xs, ks_, vs_, as_, bs_, fs_ = _sample_layer(
            xs, mod_s, state_conv_a[l], state_conv_b[l], state_ffn_conv[l], cache_kt, cache_vt, l, lw, bsz_s)
        for lst, val in zip(outs, (ks_, vs_, ap, as_, bp, bs_, fp, fs_)):
            lst.append(val)
    ys = xs.reshape(t_s, bsz_s, d).transpose(1, 0, 2)
    kp, vp = (a.reshape(a.shape[:2] + (-1, HEAD_DIM, a.shape[3])).transpose(0, 1, 4, 2, 3) for a in kept)
    return (xp, ys, kp, vp) + tuple(jnp.stack(o) for o in outs)
```

```python
import functools

import jax
import jax.numpy as jnp
from jax import lax
from jax.experimental import pallas as pl
from jax.experimental.pallas import tpu as pltpu

F32 = jnp.float32
BF16 = jnp.bfloat16

HEAD_DIM = 64
LANES = 128
SUBLANES = 8
CONV_A_WIDTH = 31
CONV_B_WIDTH = 3
FFN_CONV_WIDTH = 3
DILATED_PATTERNS = ((128, 1), (512, 4), (2048, 16))
Q_BLK = 128
CHUNK = Q_BLK // DILATED_PATTERNS[1][1]
EPS = 1e-6
NEG = -1e30
LOG2_E = 1.4426950408889634
ATTN_UNROLL = 4
VMEM_LIMIT = 56 * 1024 * 1024
PROMPT_TILE_ROWS = 512
CACHE_BUFFERS = 3
CONV_CHUNK_ROWS = 64


def _round_up(n, m):
    return (n + m - 1) // m * m


def _rms(x, g):
    return x * lax.rsqrt(jnp.mean(x * x, axis=-1, keepdims=True) + EPS) * g


def _sigmoid(x):
    return 1.0 / (1.0 + jnp.exp(-x))


def _dot(a, b):
    return jnp.dot(a, b, preferred_element_type=F32)


def _dot_nt(a, b):
    return lax.dot_general(a, b, (((1,), (1,)), ((), ())), preferred_element_type=F32)


def _const_spec(shape):
    zeros = (0,) * len(shape)
    return pl.BlockSpec(shape, lambda *_: zeros, pipeline_mode=pl.Buffered(1))


def _ada_kernel(c_ref, w_ref, b_ref, o_ref):
    c = c_ref[...]
    s = (c * _sigmoid(c)).astype(BF16)
    o_ref[0] = _dot(s, w_ref[0].astype(BF16)) + b_ref[0]


def _ada_call(c_all, w_ada, b_ada):
    depth, d, n = w_ada.shape
    rows = c_all.shape[0]
    tn = n // 4
    return pl.pallas_call(
        _ada_kernel,
        grid=(depth, n // tn),
        in_specs=[
            pl.BlockSpec((rows, d), lambda l, j: (0, 0)),
            pl.BlockSpec((1, d, tn), lambda l, j: (l, 0, j)),
            pl.BlockSpec((1, 1, tn), lambda l, j: (l, 0, j)),
        ],
        out_specs=pl.BlockSpec((1, rows, tn), lambda l, j: (l, 0, j)),
        out_shape=jax.ShapeDtypeStruct((depth, rows, n), F32),
        compiler_params=pltpu.CompilerParams(
            dimension_semantics=("arbitrary", "arbitrary"), vmem_limit_bytes=VMEM_LIMIT),
        name="ada_mod",
    )(c_all, w_ada, b_ada.reshape(depth, 1, n))


def _tap_groups(base, ntaps, shift):
    by_residue = {}
    for j in range(ntaps):
        off = base + j * shift
        by_residue.setdefault(off % SUBLANES, []).append((j, off - off % SUBLANES))
    return sorted(by_residue.items())


def _conv_chunk(buf_ref, w_ref, tap_groups, c0, chunk):
    acc = None
    for res, taps in tap_groups:
        ext = chunk + (SUBLANES if res else 0)
        part = None
        for j, off in taps:
            term = buf_ref[pl.ds(c0 + off, ext), :] * w_ref[j:j + 1, :]
            part = term if part is None else part + term
        if res:
            part = part[res:res + chunk]
        acc = part if acc is None else acc + part
    return acc


def _mixin_kernel(x_ref, mod_ref, gpre_ref, win_ref, caw_ref, cab_ref, lng_ref, lnb_ref, cbw_ref,
                  goa_ref, gob_ref, sta_ref, stb_ref, *rest,
                  rows, shift, d_a, d_b, d_c, carry_a, carry_b, n_tiles, first_kept_tile, n_prev):
    prev_refs, rest = (rest[:2], rest[2:]) if n_prev else ((), rest)
    mixab_ref, q_ref, k_ref, v_ref, na_ref, nb_ref = rest[:6]
    abuf, ubuf = rest[-2:]
    t = pl.program_id(1)

    @pl.when(t == 0)
    def _():
        abuf[0:carry_a, :] = sta_ref[0]
        ubuf[0:carry_b, :] = stb_ref[0]

    x = x_ref[0]
    h = (_rms(x, gpre_ref[...]) * (1.0 + mod_ref[0, 1]) + mod_ref[0, 0]).astype(BF16)

    o_b = 2 * d_a
    o_c = o_b + 3 * d_b
    za = _dot(h, win_ref[:, 0:2 * d_a])
    abuf[carry_a:carry_a + rows, :] = za[:, 0:d_a] * _sigmoid(za[:, d_a:2 * d_a])
    zb = _dot(h, win_ref[:, o_b:o_b + 3 * d_b])
    ubuf[carry_b:carry_b + rows, :] = zb[:, 2 * d_b:3 * d_b] * zb[:, 0:d_b]
    b_gate = zb[:, d_b:2 * d_b]

    q = _dot(h, win_ref[:, o_c:o_c + d_c])
    k = _dot(h, win_ref[:, o_c + d_c:o_c + 2 * d_c])
    v = _dot(h, win_ref[:, o_c + 2 * d_c:o_c + 3 * d_c])
    for ref, val in ((q_ref, q), (k_ref, k), (v_ref, v)):
        for grp in range(d_c // LANES):
            ref[0, grp] = val[:, grp * LANES:(grp + 1) * LANES]

    taps_a = _tap_groups(carry_a - (CONV_A_WIDTH - 1) * shift, CONV_A_WIDTH, shift)
    chunk = min(rows, CONV_CHUNK_ROWS)
    a = jnp.concatenate([_conv_chunk(abuf, caw_ref, taps_a, c0, chunk) for c0 in range(0, rows, chunk)], axis=0)
    a = a + cab_ref[...]
    mu = jnp.mean(a, axis=-1, keepdims=True)
    ac = a - mu
    var = jnp.mean(ac * ac, axis=-1, keepdims=True)
    a = ac * lax.rsqrt(var + EPS) * lng_ref[...] + lnb_ref[...]
    a = a * _sigmoid(a)
    mixab_ref[0, :, 0:d_a] = _rms(a, goa_ref[...]).astype(BF16)
    na_ref[0] = abuf[rows:rows + carry_a, :]

    taps_b = _tap_groups(carry_b - (CONV_B_WIDTH - 1) * shift, CONV_B_WIDTH, shift)
    u = _conv_chunk(ubuf, cbw_ref, taps_b, 0, rows)
    mixab_ref[0, :, d_a:d_a + d_b] = _rms(b_gate * u, gob_ref[...]).astype(BF16)
    nb_ref[0] = ubuf[rows:rows + carry_b, :]

    if first_kept_tile is not None:
        kept_kt_ref, kept_vt_ref = rest[6:8]

        @pl.when(t >= first_kept_tile)
        def _():
            for idx, (kept_ref, new) in enumerate(((kept_kt_ref, k), (kept_vt_ref, v))):
                for i in range(n_prev):
                    kept_ref[i, 0] = prev_refs[idx][i, 0]
                kept_ref[n_prev, 0] = new.T

    if n_tiles > 1:
        abuf[0:carry_a, :] = abuf[rows:rows + carry_a, :]
        ubuf[0:carry_b, :] = ubuf[rows:rows + carry_b, :]


def _mixin_call(x, mod, st_a, st_b, lw, *, rows, shift, keep, prev_kept=()):
    groups, n, d = x.shape
    rm = mod.shape[2]
    d_a = lw['conv_a_w'].shape[1]
    d_b = lw['conv_b_w'].shape[1]
    d_c = lw['g_out_c'].shape[1]
    p_in = lw['w_in'].shape[1]
    carry_a, carry_b = st_a.shape[1], st_b.shape[1]
    n_tiles = n // rows
    assert keep % rows == 0 and keep <= n
    first_kept_tile = (n - keep) // rows if keep < n else None
    n_prev = prev_kept[0].shape[0] if prev_kept else 0
    kern = functools.partial(_mixin_kernel, rows=rows, shift=shift, d_a=d_a, d_b=d_b, d_c=d_c,
                             carry_a=carry_a, carry_b=carry_b, n_tiles=n_tiles, first_kept_tile=first_kept_tile,
                             n_prev=n_prev)
    kept_specs, kept_shapes, prev_specs = [], [], []
    if first_kept_tile is not None:
        kept_map = lambda g, t: (0, g, 0, jnp.maximum(t - first_kept_tile, 0))
        kept_specs = [pl.BlockSpec((n_prev + 1, 1, d_c, rows), kept_map)] * 2
        kept_shapes = [jax.ShapeDtypeStruct((n_prev + 1, groups, d_c, keep), F32)] * 2
        prev_specs = [pl.BlockSpec((n_prev, 1, d_c, rows), kept_map)] * (2 if n_prev else 0)
    tile = lambda w: pl.BlockSpec((1, rows, w), lambda g, t: (g, t, 0))
    per_group = lambda r, w: pl.BlockSpec((1, r, w), lambda g, t: (g, 0, 0))
    grouped = pl.BlockSpec((1, d_c // LANES, rows, LANES), lambda g, t: (g, 0, t, 0))
    outs = pl.pallas_call(
        kern,
        grid=(groups, n_tiles),
        in_specs=[
            tile(d),
            pl.BlockSpec((1, 6, rm, d), lambda g, t: (g, 0, 0, 0)),
            _const_spec((1, d)),
            _const_spec((d, p_in)),
            _const_spec((CONV_A_WIDTH, d_a)),
            _const_spec((1, d_a)),
            _const_spec((1, d_a)),
            _const_spec((1, d_a)),
            _const_spec((CONV_B_WIDTH, d_b)),
            _const_spec((1, d_a)),
            _const_spec((1, d_b)),
            per_group(carry_a, d_a),
            per_group(carry_b, d_b),
        ] + prev_specs,
        out_specs=[
            tile(d_a + d_b), grouped, grouped, grouped,
            per_group(carry_a, d_a), per_group(carry_b, d_b),
        ] + kept_specs,
        out_shape=[
            jax.ShapeDtypeStruct((groups, n, d_a + d_b), BF16),
            jax.ShapeDtypeStruct((groups, d_c // LANES, n, LANES), F32),
            jax.ShapeDtypeStruct((groups, d_c // LANES, n, LANES), F32),
            jax.ShapeDtypeStruct((groups, d_c // LANES, n, LANES), F32),
            jax.ShapeDtypeStruct((groups, carry_a, d_a), F32),
            jax.ShapeDtypeStruct((groups, carry_b, d_b), F32),
        ] + kept_shapes,
        scratch_shapes=[
            pltpu.VMEM((carry_a + rows, d_a), F32),
            pltpu.VMEM((carry_b + rows, d_b), F32),
        ],
        compiler_params=pltpu.CompilerParams(
            dimension_semantics=("arbitrary", "arbitrary"), vmem_limit_bytes=VMEM_LIMIT),
        name="mix_in",
    )(x, mod, lw['g_pre_mix'], lw['w_in'], lw['conv_a_w'], lw['conv_a_b'], lw['ln_a_g'], lw['ln_a_b'],
      lw['conv_b_w'], lw['g_out_a'], lw['g_out_b'], st_a, st_b, *prev_kept)
    return outs


def _attn_kernel(q_ref, k_ref, v_ref, o_ref, qs0, qs1, kts, vs, acc, mrun, lrun, bias, pbuf, mbuf,
                 qmid, kmid, vmid, *, seq):
    n_blk = seq // Q_BLK
    lane = lax.broadcasted_iota(jnp.int32, (1, LANES), 1)
    head0 = lane < HEAD_DIM
    scale = HEAD_DIM ** -0.5 * LOG2_E
    mid = DILATED_PATTERNS[1][1]
    n_mid = seq // mid

    @pl.when(jnp.logical_and(pl.program_id(0) == 0, pl.program_id(1) == 0))
    def _():
        idx = lax.broadcasted_iota(jnp.int32, (Q_BLK, 2 * Q_BLK), 0)
        col = lax.broadcasted_iota(jnp.int32, (Q_BLK, 2 * Q_BLK), 1)
        for perm, row in enumerate((idx, (idx % CHUNK) * mid + idx // CHUNK)):
            band = jnp.logical_and(col >= row, col <= row + Q_BLK)
            bias[perm, 0] = jnp.where(band, 0.0, NEG)
            bias[perm, 1] = jnp.where(jnp.logical_and(band, col >= Q_BLK), 0.0, NEG)
        vs[0:Q_BLK, 0:LANES] = jnp.zeros((Q_BLK, LANES), BF16)
        vs[:, LANES:2 * LANES] = jnp.ones((seq + Q_BLK, LANES), BF16)
        kts[0] = jnp.zeros((LANES, Q_BLK), BF16)

    blk_per_mid = n_blk // mid
    staged = ((q_ref.at[0, 0], qmid), (k_ref.at[0, 0], kmid), (v_ref.at[0, 0], vmid))

    def stage(jb, carry):
        phase = jb // blk_per_mid
        src = pl.ds(phase + (jb - phase * blk_per_mid) * (Q_BLK * mid), Q_BLK, stride=mid)
        dst = pl.ds(pl.multiple_of(jb * Q_BLK, Q_BLK), Q_BLK)
        for ref, buf in staged:
            buf[dst, :] = ref[src, :]
        return carry

    lax.fori_loop(0, n_blk, stage, 0, unroll=ATTN_UNROLL)

    order = DILATED_PATTERNS[::-1]
    n_grp = n_blk // ATTN_UNROLL
    for step, (_, dil) in enumerate(order):
        blk_per_phase = n_blk // dil

        def mid_rows(j, dil=dil, blk_per_phase=blk_per_phase):
            if dil == 1:
                return [pl.ds(pl.multiple_of(c * n_mid + j * CHUNK, CHUNK), CHUNK) for c in range(mid)]
            if dil == mid:
                return [pl.ds(pl.multiple_of(j * Q_BLK, Q_BLK), Q_BLK)]
            ratio = dil // mid
            phase = j // blk_per_phase
            start = (phase % mid) * n_mid + phase // mid + (j - phase * blk_per_phase) * (Q_BLK * ratio)
            return [pl.ds(start, Q_BLK, stride=ratio)]

        def load_rows(ref, pieces):
            parts = [ref[p, :] for p in pieces]
            return parts[0] if len(parts) == 1 else jnp.concatenate(parts, axis=0)

        def store_rows(ref, pieces, val):
            n = Q_BLK // len(pieces)
            for c, p in enumerate(pieces):
                ref[p, :] = val[c * n:(c + 1) * n]

        def split(g, dil=dil, mid_rows=mid_rows, load_rows=load_rows):
            g = jnp.minimum(g, n_grp - 1)
            for u in range(ATTN_UNROLL):
                j = g * ATTN_UNROLL + u
                pieces = mid_rows(j)
                dst = pl.ds(pl.multiple_of(j * Q_BLK, Q_BLK), Q_BLK)
                dst_kv = pl.ds(pl.multiple_of((j + 1) * Q_BLK, Q_BLK), Q_BLK)
                qv = (load_rows(qmid, pieces) * scale).astype(BF16)
                qs0[dst, :] = qv[:, :HEAD_DIM]
                qs1[dst, :] = qv[:, HEAD_DIM:]
                if dil == 1:
                    k_blk, v_blk = k_ref[0, 0, dst, :], v_ref[0, 0, dst, :]
                else:
                    k_blk, v_blk = load_rows(kmid, pieces), load_rows(vmid, pieces)
                kts[j + 1] = k_blk.T.astype(BF16)
                vs[dst_kv, 0:LANES] = v_blk.astype(BF16)

        def scores(g, slot, dil=dil, blk_per_phase=blk_per_phase):
            g = jnp.minimum(g, n_grp - 1)
            for u in range(ATTN_UNROLL):
                j = g * ATTN_UNROLL + u
                qrows = pl.ds(pl.multiple_of(j * Q_BLK, Q_BLK), Q_BLK)
                kbt = jnp.concatenate([kts[j], kts[j + 1]], axis=1)
                bb = bias[1 if dil == 1 else 0, (j % blk_per_phase == 0).astype(jnp.int32)]
                ms = []
                for h, qs in enumerate((qs0, qs1)):
                    s = _dot(qs[qrows, :], kbt[h * HEAD_DIM:(h + 1) * HEAD_DIM, :]) + bb
                    m = jnp.max(s, axis=-1, keepdims=True)
                    pbuf[slot, u, h] = jnp.exp2(s - m).astype(BF16)
                    ms.append(m)
                mbuf[slot, u] = jnp.where(head0, ms[0], ms[1])

        def values(g, slot, step=step, mid_rows=mid_rows, load_rows=load_rows, store_rows=store_rows):
            for u in range(ATTN_UNROLL):
                j = g * ATTN_UNROLL + u
                vb = vs[pl.ds(pl.multiple_of(j * Q_BLK, Q_BLK), 2 * Q_BLK), :]
                pv0 = _dot(pbuf[slot, u, 0], vb)
                pv1 = _dot(pbuf[slot, u, 1], vb)
                a_new = jnp.where(head0, pv0[:, :LANES], pv1[:, :LANES])
                l_new = jnp.where(head0, pv0[:, LANES:], pv1[:, LANES:])
                m_new = mbuf[slot, u]
                dst = mid_rows(j)
                if step > 0:
                    m_old = load_rows(mrun, dst)
                    m_tot = jnp.maximum(m_old, m_new)
                    w_old = jnp.exp2(m_old - m_tot)
                    w_new = jnp.exp2(m_new - m_tot)
                    a_new = w_old * load_rows(acc, dst) + w_new * a_new
                    l_new = w_old * load_rows(lrun, dst) + w_new * l_new
                    m_new = m_tot
                if step < len(order) - 1:
                    store_rows(mrun, dst, m_new)
                    store_rows(lrun, dst, l_new)
                    store_rows(acc, dst, a_new)
                else:
                    out = [pl.ds(j * Q_BLK + c, CHUNK, stride=mid) for c in range(mid)]
                    store_rows(o_ref.at[0, 0], out, a_new / l_new)

        for g in range(3):
            split(jnp.int32(g))
        scores(jnp.int32(0), 0)

        def pair(t, carry, split=split, scores=scores, values=values):
            g = 2 * t
            scores(g + 1, 1)
            values(g, 0)
            scores(g + 2, 0)
            values(g + 1, 1)
            split(g + 3)
            split(g + 4)
            return carry

        lax.fori_loop(0, n_grp // 2 - 1, pair, 0)
        scores(jnp.int32(n_grp - 1), 1)
        values(jnp.int32(n_grp - 2), 0)
        values(jnp.int32(n_grp - 1), 1)


def _attn_call(q, k, v):
    bsz, n_grp, seq, _ = q.shape
    assert seq % (Q_BLK * DILATED_PATTERNS[-1][1]) == 0 and seq % (2 * ATTN_UNROLL * Q_BLK) == 0
    assert len(DILATED_PATTERNS) == 3 and DILATED_PATTERNS[0][1] == 1
    assert DILATED_PATTERNS[2][1] % DILATED_PATTERNS[1][1] == 0
    spec = pl.BlockSpec((1, 1, seq, LANES), lambda b, h: (b, h, 0, 0))
    return pl.pallas_call(
        functools.partial(_attn_kernel, seq=seq),
        grid=(bsz, n_grp),
        in_specs=[spec, spec, spec],
        out_specs=spec,
        out_shape=jax.ShapeDtypeStruct(q.shape, F32),
        scratch_shapes=[
            pltpu.VMEM((seq, HEAD_DIM), BF16),
            pltpu.VMEM((seq, HEAD_DIM), BF16),
            pltpu.VMEM((seq // Q_BLK + 1, LANES, Q_BLK), BF16),
            pltpu.VMEM((seq + Q_BLK, 2 * LANES), BF16),
            pltpu.VMEM((seq, LANES), F32),
            pltpu.VMEM((seq, LANES), F32),
            pltpu.VMEM((seq, LANES), F32),
            pltpu.VMEM((2, 2, Q_BLK, 2 * Q_BLK), F32),
            pltpu.VMEM((2, ATTN_UNROLL, 2, Q_BLK, 2 * Q_BLK), BF16),
            pltpu.VMEM((2, ATTN_UNROLL, Q_BLK, LANES), F32),
            pltpu.VMEM((seq, LANES), F32),
            pltpu.VMEM((seq, LANES), F32),
            pltpu.VMEM((seq, LANES), F32),
        ],
        compiler_params=pltpu.CompilerParams(
            dimension_semantics=("arbitrary", "arbitrary"), vmem_limit_bytes=VMEM_LIMIT),
        name="attn_prompt",
    )(q, k, v)


def _multiplicity(dist):
    cnt = jnp.zeros(dist.shape, F32)
    for window, dil in DILATED_PATTERNS:
        hit = jnp.logical_and(dist >= 0, jnp.logical_and(dist <= window, dist % dil == 0))
        cnt = cnt + jnp.where(hit, 1.0, 0.0)
    return cnt


def _attn_sample_kernel(q_ref, kn_ref, vn_ref, kt_hbm, vt_hbm, o_ref, cnt_c, cnt_n, kbuf, vbuf, sem, *,
                        n_q, n_heads, w_buf, layer, bsz):
    b = pl.program_id(0)
    ahead = CACHE_BUFFERS - 1

    def fetch(seq):
        slot = seq % CACHE_BUFFERS
        return (pltpu.make_async_copy(kt_hbm.at[layer, seq], kbuf.at[slot], sem.at[0, slot]),
                pltpu.make_async_copy(vt_hbm.at[layer, seq], vbuf.at[slot], sem.at[1, slot]))

    @pl.when(b == 0)
    def _():
        for seq in range(min(ahead, bsz)):
            for cp in fetch(seq):
                cp.start()

    @pl.when(b + ahead < bsz)
    def _():
        for cp in fetch(b + ahead):
            cp.start()

    for cp in fetch(b):
        cp.wait()
    kt_ref = kbuf.at[b % CACHE_BUFFERS]
    vt_ref = vbuf.at[b % CACHE_BUFFERS]

    @pl.when(b == 0)
    def _():
        def table(n_keys, first_pos):
            q_pos = w_buf + lax.broadcasted_iota(jnp.int32, (n_q, n_keys), 0)
            k_pos = first_pos + lax.broadcasted_iota(jnp.int32, (n_q, n_keys), 1)
            return _multiplicity(q_pos - k_pos)

        cnt_c[...] = table(w_buf, 0)
        cnt_n[...] = table(n_q, w_buf)

    mult_c = cnt_c[...]
    mult_n = cnt_n[...]
    for h in range(n_heads):
        q = (q_ref[0, h] * HEAD_DIM ** -0.5).astype(BF16)
        s_c = jnp.where(mult_c > 0.0, _dot(q, kt_ref[h].astype(BF16)), NEG)
        s_n = jnp.where(mult_n > 0.0, _dot_nt(q, kn_ref[0, h].astype(BF16)), NEG)
        m = jnp.maximum(jnp.max(s_c, axis=-1, keepdims=True), jnp.max(s_n, axis=-1, keepdims=True))
        p_c = mult_c * jnp.exp(s_c - m)
        p_n = mult_n * jnp.exp(s_n - m)
        l = jnp.sum(p_c, axis=-1, keepdims=True) + jnp.sum(p_n, axis=-1, keepdims=True)
        o = _dot_nt(p_c.astype(BF16), vt_ref[h].astype(BF16)) + _dot(p_n.astype(BF16), vn_ref[0, h].astype(BF16))
        o_ref[0, h] = o / l


def _attn_sample_call(q, k_new, v_new, cache_kt, cache_vt, layer):
    bsz, n_heads, n_q, head_dim = q.shape
    w_buf = cache_kt.shape[4]
    new_spec = pl.BlockSpec((1, n_heads, n_q, head_dim), lambda b: (b, 0, 0, 0))
    cache_spec = pl.BlockSpec(memory_space=pl.ANY)
    cache_buf = pltpu.VMEM((CACHE_BUFFERS, n_heads, head_dim, w_buf), F32)
    return pl.pallas_call(
        functools.partial(_attn_sample_kernel, n_q=n_q, n_heads=n_heads, w_buf=w_buf, layer=layer, bsz=bsz),
        grid=(bsz,),
        in_specs=[new_spec, new_spec, new_spec, cache_spec, cache_spec],
        out_specs=new_spec,
        out_shape=jax.ShapeDtypeStruct(q.shape, F32),
        scratch_shapes=[pltpu.VMEM((n_q, w_buf), F32), pltpu.VMEM((n_q, n_q), F32), cache_buf, cache_buf,
                        pltpu.SemaphoreType.DMA((2, CACHE_BUFFERS))],
        compiler_params=pltpu.CompilerParams(
            dimension_semantics=("arbitrary",), vmem_limit_bytes=VMEM_LIMIT),
        name="attn_sample",
    )(q, k_new, v_new, cache_kt, cache_vt)


def _outffn_kernel(mixab_ref, o_ref, x_ref, mod_ref, goc_ref, wo_ref, gpm_ref, gpf_ref, wg_ref, wu_ref,
                   cfw_ref, wd_ref, gpo_ref, stf_ref,
                   y_ref, nf_ref, gbuf, *, rows, shift, carry_f, n_tiles):
    t = pl.program_id(1)

    @pl.when(t == 0)
    def _():
        gbuf[0:carry_f, :] = stf_ref[0]

    d_ab = mixab_ref.shape[2]
    o = jnp.concatenate([o_ref[0, grp] for grp in range(o_ref.shape[1])], axis=-1)
    oc = _rms(o, goc_ref[...]).astype(BF16)
    y = _dot(mixab_ref[0], wo_ref[0:d_ab, :]) + _dot(oc, wo_ref[d_ab:, :])
    x1 = x_ref[0] + mod_ref[0, 2] * _rms(y, gpm_ref[...])

    h = (_rms(x1, gpf_ref[...]) * (1.0 + mod_ref[0, 4]) + mod_ref[0, 3]).astype(BF16)
    gbuf[carry_f:carry_f + rows, :] = _dot(h, wg_ref[...])
    taps = _tap_groups(carry_f - (FFN_CONV_WIDTH - 1) * shift, FFN_CONV_WIDTH, shift)
    g = _conv_chunk(gbuf, cfw_ref, taps, 0, rows)
    f = (g * _sigmoid(g) * _dot(h, wu_ref[...])).astype(BF16)
    y2 = _dot(f, wd_ref[...])
    y_ref[0] = x1 + mod_ref[0, 5] * _rms(y2, gpo_ref[...])
    nf_ref[0] = gbuf[rows:rows + carry_f, :]
    if n_tiles > 1:
        gbuf[0:carry_f, :] = gbuf[rows:rows + carry_f, :]


def _outffn_call(mixab, o, x, mod, st_f, lw, *, rows, shift):
    groups, n, d = x.shape
    rm = mod.shape[2]
    d_ab = mixab.shape[2]
    n_og = o.shape[1]
    d_c = n_og * LANES
    d_ff = lw['w_gate'].shape[1]
    carry_f = st_f.shape[1]
    n_tiles = n // rows
    kern = functools.partial(_outffn_kernel, rows=rows, shift=shift, carry_f=carry_f, n_tiles=n_tiles)
    tile = lambda w: pl.BlockSpec((1, rows, w), lambda g, t: (g, t, 0))
    per_group = lambda r, w: pl.BlockSpec((1, r, w), lambda g, t: (g, 0, 0))
    return pl.pallas_call(
        kern,
        grid=(groups, n_tiles),
        in_specs=[
            tile(d_ab), pl.BlockSpec((1, n_og, rows, LANES), lambda g, t: (g, 0, t, 0)), tile(d),
            pl.BlockSpec((1, 6, rm, d), lambda g, t: (g, 0, 0, 0)),
            _const_spec((1, d_c)),
            _const_spec((d_ab + d_c, d)),
            _const_spec((1, d)),
            _const_spec((1, d)),
            _const_spec((d, d_ff)),
            _const_spec((d, d_ff)),
            _const_spec((FFN_CONV_WIDTH, d_ff)),
            _const_spec((d_ff, d)),
            _const_spec((1, d)),
            per_group(carry_f, d_ff),
        ],
        out_specs=[tile(d), per_group(carry_f, d_ff)],
        out_shape=[
            jax.ShapeDtypeStruct((groups, n, d), F32),
            jax.ShapeDtypeStruct((groups, carry_f, d_ff), F32),
        ],
        scratch_shapes=[pltpu.VMEM((carry_f + rows, d_ff), F32)],
        compiler_params=pltpu.CompilerParams(
            dimension_semantics=("arbitrary", "arbitrary"), vmem_limit_bytes=VMEM_LIMIT),
        name="out_ffn",
    )(mixab, o, x, mod, lw['g_out_c'], lw['w_o'], lw['g_post_mix'], lw['g_pre_ffn'], lw['w_gate'],
      lw['w_up'], lw['conv_f_w'], lw['w_down'], lw['g_post_ffn'], st_f)


def _front_pad(state, rows):
    return jnp.pad(state, ((0, 0), (rows - state.shape[1], 0), (0, 0)))


def _prompt_layer(x, mod, lw, prev_kept):
    bsz, seq, d = x.shape
    d_a, d_b = lw['conv_a_w'].shape[1], lw['conv_b_w'].shape[1]
    d_ff = lw['w_gate'].shape[1]
    rows = PROMPT_TILE_ROWS if seq % PROMPT_TILE_ROWS == 0 else seq
    carry_a = _round_up(CONV_A_WIDTH - 1, SUBLANES)
    carry_b = _round_up(CONV_B_WIDTH - 1, SUBLANES)
    carry_f = _round_up(FFN_CONV_WIDTH - 1, SUBLANES)
    mod4 = mod.reshape(bsz, 6, 1, d)
    keep = min(DILATED_PATTERNS[-1][0], seq)
    assert keep < seq
    mixab, q, k, v, na, nb, kept_kt, kept_vt = _mixin_call(
        x, mod4, jnp.zeros((bsz, carry_a, d_a), F32), jnp.zeros((bsz, carry_b, d_b), F32), lw,
        rows=rows, shift=1, keep=keep, prev_kept=prev_kept)
    o = _attn_call(q, k, v)
    y, nf = _outffn_call(mixab, o, x, mod4, jnp.zeros((bsz, carry_f, d_ff), F32), lw, rows=rows, shift=1)
    return (y, (kept_kt, kept_vt), na[:, carry_a - (CONV_A_WIDTH - 1):], nb[:, carry_b - (CONV_B_WIDTH - 1):],
            nf[:, carry_f - (FFN_CONV_WIDTH - 1):])


def _time_major(state):
    bsz, k, c = state.shape
    return state.transpose(1, 0, 2).reshape(1, k * bsz, c)


def _batch_major(rows, bsz):
    _, n, c = rows.shape
    return rows.reshape(n // bsz, bsz, c).transpose(1, 0, 2)


def _sample_layer(x_tm, mod_tm, st_a, st_b, st_f, cache_kt, cache_vt, layer, lw, bsz):
    n = x_tm.shape[1]
    carry_a = _round_up((CONV_A_WIDTH - 1) * bsz, SUBLANES)
    carry_b = _round_up((CONV_B_WIDTH - 1) * bsz, SUBLANES)
    carry_f = _round_up((FFN_CONV_WIDTH - 1) * bsz, SUBLANES)
    mixab, q, k, v, na, nb = _mixin_call(
        x_tm, mod_tm, _front_pad(_time_major(st_a), carry_a), _front_pad(_time_major(st_b), carry_b), lw,
        rows=n, shift=bsz, keep=n)
    ungroup = lambda a: a.transpose(0, 2, 1, 3).reshape(1, n, -1)
    qb, kb, vb = (_batch_major(ungroup(a), bsz) for a in (q, k, v))
    n_q, n_heads = qb.shape[1], qb.shape[2] // HEAD_DIM
    per_head = lambda a: a.reshape(bsz, n_q, n_heads, HEAD_DIM).transpose(0, 2, 1, 3)
    o = _attn_sample_call(per_head(qb), per_head(kb), per_head(vb), cache_kt, cache_vt, layer)
    o_tm = o.transpose(2, 0, 1, 3).reshape(1, n, -1, LANES).transpose(0, 2, 1, 3)
    y, nf = _outffn_call(mixab, o_tm, x_tm, mod_tm, _front_pad(_time_major(st_f), carry_f), lw,
                         rows=n, shift=bsz)
    new_k = kb.reshape(bsz, kb.shape[1], n_heads, HEAD_DIM)
    new_v = vb.reshape(bsz, vb.shape[1], n_heads, HEAD_DIM)
    new_a = _batch_major(na[:, carry_a - (CONV_A_WIDTH - 1) * bsz:], bsz)
    new_b = _batch_major(nb[:, carry_b - (CONV_B_WIDTH - 1) * bsz:], bsz)
    new_f = _batch_major(nf[:, carry_f - (FFN_CONV_WIDTH - 1) * bsz:], bsz)
    return y, new_k, new_v, new_a, new_b, new_f


def kernel(x_prompt, x_sample, cache_k, cache_v, state_conv_a, state_conv_b, state_ffn_conv, c_prompt, c_sample, w_ada, b_ada, g_pre_mix, w_in, conv_a_w, conv_a_b, ln_a_g, ln_a_b, conv_b_w, g_out_a, g_out_b, g_out_c, w_o, g_post_mix, g_pre_ffn, w_gate, w_up, conv_f_w, w_down, g_post_ffn):
    depth = w_ada.shape[0]
    bsz_p, _, d = x_prompt.shape
    bsz_s, t_s, _ = x_sample.shape

    mod = _ada_call(jnp.concatenate([c_prompt, c_sample], axis=0), w_ada, b_ada)

    cache_kt = cache_k.transpose(0, 1, 3, 4, 2)
    cache_vt = cache_v.transpose(0, 1, 3, 4, 2)

    xp = x_prompt
    xs = x_sample.transpose(1, 0, 2).reshape(1, t_s * bsz_s, d)
    outs = [[] for _ in range(8)]
    kept = ()
    for l in range(depth):
        row = lambda a: a[l][None, :]
        lw = {
            'g_pre_mix': row(g_pre_mix), 'w_in': w_in[l].astype(BF16),
            'conv_a_w': conv_a_w[l], 'conv_a_b': row(conv_a_b), 'ln_a_g': row(ln_a_g), 'ln_a_b': row(ln_a_b),
            'conv_b_w': conv_b_w[l], 'g_out_a': row(g_out_a), 'g_out_b': row(g_out_b), 'g_out_c': row(g_out_c),
            'w_o': w_o[l].astype(BF16), 'g_post_mix': row(g_post_mix), 'g_pre_ffn': row(g_pre_ffn),
            'w_gate': w_gate[l].astype(BF16), 'w_up': w_up[l].astype(BF16), 'conv_f_w': conv_f_w[l],
            'w_down': w_down[l].astype(BF16), 'g_post_ffn': row(g_post_ffn),
        }
        mod_p = mod[l, :bsz_p].reshape(bsz_p, 6, d)
        mod_s = mod[l, bsz_p:].reshape(bsz_s, 6, d).transpose(1, 0, 2)
        mod_s = jnp.tile(mod_s[:, None], (1, t_s, 1, 1)).reshape(1, 6, t_s * bsz_s, d)

        xp, kept, ap, bp, fp = _prompt_layer(xp, mod_p, lw, kept)
        xs, ks_, vs_, as_, bs_, fs_ = _sample_layer(
            xs, mod_s, state_conv_a[l], state_conv_b[l], state_ffn_conv[l], cache_kt, cache_vt, l, lw, bsz_s)
        for lst, val in zip(outs, (ks_, vs_, ap, as_, bp, bs_, fp, fs_)):
            lst.append(val)
    ys = xs.reshape(t_s, bsz_s, d).transpose(1, 0, 2)
    kp, vp = (a.reshape(a.shape[:2] + (-1, HEAD_DIM, a.shape[3])).transpose(0, 1, 4, 2, 3) for a in kept)
    return (xp, ys, kp, vp) + tuple(jnp.stack(o) for o in outs)
```
